```python
import math
import jax
import jax.numpy as jnp
from jax import lax
import numpy as np

D_MODEL = 1024
BATCH = 32
SEQ = 256
DEPTH = 2
DEC_BATCH = 8
DEC_SEQ = 2048
PAST_LEN = 256

GRID_W = 64
N_BRANCH = 4
BRANCH_W = 256
DA_H = 4
DA_DK = 32
DA_DV = 64
RW_H = 4
RW_N = 64
RW_W = RW_H * RW_N
RW_DECAY_LORA = 64
RW_A_LORA = 64
RW_G_LORA = 128
ML_H = 4
ML_DK = 32
ML_DV = 64
ML_CHUNK = 64
GL_H = 4
GL_DK = 32
GL_DV = 64
GL_ALPHA_LORA = 16
GL_TAU = 16.0
GL_CHUNK = 64
N_EXPERTS = 32
TOP_K = 4
D_FF = 1024
SWIGLU_ALPHA = 1.702
SWIGLU_LIMIT = 7.0
MOE_BLOCK = 128
Q_BLOCK = 128
ROPE_BASE = 10000.0
EPS = 1e-6

CONV_SPLITS = (('rw_r', RW_W), ('rw_k', RW_W), ('rw_v', RW_W), ('rw_wd', RW_DECAY_LORA), ('rw_ad', RW_A_LORA),
               ('rw_gd', RW_G_LORA), ('ml_q', ML_H * ML_DK), ('ml_k', ML_H * ML_DK),
               ('gl_q', GL_H * GL_DK), ('gl_k', GL_H * GL_DK))
PLAIN_SPLITS = (('da_q', DA_H * 2 * DA_DK), ('da_k', DA_H * 2 * DA_DK), ('da_v', DA_H * DA_DV),
                ('ml_v', ML_H * ML_DV), ('ml_o', ML_H * ML_DV), ('ml_i', 2 * ML_H), ('ml_f', 2 * ML_H),
                ('gl_v', GL_H * GL_DV), ('gl_a', GL_ALPHA_LORA), ('gl_g', GL_H * GL_DV))
IN_SPLITS = CONV_SPLITS + PLAIN_SPLITS
IN_NAMES = tuple(n for n, _ in IN_SPLITS)
IN_SIZES = tuple(s for _, s in IN_SPLITS)
N_CONV = sum(s for _, s in CONV_SPLITS)
N_IN = sum(IN_SIZES)
IN_OFFSETS = tuple(int(o) for o in np.cumsum(IN_SIZES)[:-1])

kernel_name = 'hybrid_diffattn_rwkv7_mlstm_gla_moe_dit_step'


def rms_norm(x, g):
    xf = x.astype(jnp.float32)
    y = xf * lax.rsqrt(jnp.mean(xf * xf, axis=-1, keepdims=True) + EPS)
    return (y * g.astype(jnp.float32)).astype(x.dtype)


def head_norm(x, g, center=False):
    xf = x.astype(jnp.float32)
    if center:
        xf = xf - jnp.mean(xf, axis=-1, keepdims=True)
    y = xf * lax.rsqrt(jnp.mean(xf * xf, axis=-1, keepdims=True) + EPS)
    return y * g.astype(jnp.float32).reshape(x.shape[-2], x.shape[-1])


def centred_conv3(x, w):
    xp = jnp.pad(x, ((0, 0), (1, 1), (0, 0)))
    return xp[:, :-2] * w[0] + xp[:, 1:-1] * w[1] + xp[:, 2:] * w[2]


def rope_1d(x, pos):
    half = x.shape[-1] // 2
    freqs = ROPE_BASE ** (-jnp.arange(half, dtype=jnp.float32) / half)
    ang = pos.astype(jnp.float32)[:, None] * freqs
    shp = (1, x.shape[1]) + (1,) * (x.ndim - 3) + (half,)
    cos = jnp.cos(ang).reshape(shp)
    sin = jnp.sin(ang).reshape(shp)
    x1, x2 = x[..., :half], x[..., half:]
    return jnp.concatenate([x1 * cos - x2 * sin, x1 * sin + x2 * cos], axis=-1)


def axial_rope_2d(x):
    n_tok = x.shape[1]
    n_rows = n_tok // GRID_W
    row = jnp.repeat(jnp.arange(n_rows), GRID_W)
    col = jnp.tile(jnp.arange(GRID_W), n_rows)
    h = x.shape[-1] // 2
    return jnp.concatenate([rope_1d(x[..., :h], row), rope_1d(x[..., h:], col)], axis=-1)


def to_chunks(a, ch):
    B, L = a.shape[:2]
    a = a.reshape((B, L // ch, ch) + a.shape[2:])
    a = jnp.moveaxis(a, 3, 2)
    return jnp.moveaxis(a, 1, 0)


def from_chunks(a):
    nc, B, H, ch = a.shape[:4]
    a = jnp.moveaxis(jnp.moveaxis(a, 0, 1), 3, 2)
    return a.reshape((B, nc * ch, H) + a.shape[4:])


def diff_attention(q, k, v, lam):
    B, Lq = q.shape[:2]
    nb = Lq // Q_BLOCK
    qb = jnp.moveaxis(q.reshape((B, nb, Q_BLOCK) + q.shape[2:]), 1, 0) * (DA_DK ** -0.5)

    def one_block(qi):
        s = jnp.einsum('bqhmd,bkhmd->bhmqk', qi, k)
        pr = jax.nn.softmax(s, axis=-1)
        a = pr[:, :, 0] - lam * pr[:, :, 1]
        return jnp.einsum('bhqk,bkhd->bqhd', a, v)

    o = lax.map(one_block, qb)
    return jnp.moveaxis(o, 0, 1).reshape(B, Lq, q.shape[2], v.shape[-1])


def rwkv7_scan(S0, r, w, kk, b, k, v):
    def step(S, inp):
        r_t, w_t, kk_t, b_t, k_t, v_t = inp
        sa = jnp.einsum('bhvk,bhk->bhv', S, -kk_t)
        S = S * w_t[:, :, None, :] + sa[..., None] * b_t[:, :, None, :] + v_t[..., None] * k_t[:, :, None, :]
        return S, jnp.einsum('bhvk,bhk->bhv', S, r_t)

    S, y = lax.scan(step, S0, tuple(jnp.moveaxis(a, 1, 0) for a in (r, w, kk, b, k, v)))
    return jnp.moveaxis(y, 0, 1), S


def rwkv_branch(pc, S0, i, p):
    B, L, _ = pc['rw_r'].shape
    hd = lambda a: a.reshape(B, L, RW_H, RW_N)
    r = hd(pc['rw_r'])
    v = hd(pc['rw_v'])
    k_base = pc['rw_k']
    kk = hd(k_base * p['rw_k_k'][i])
    kk = kk * lax.rsqrt(jnp.sum(kk * kk, axis=-1, keepdims=True) + EPS)
    r_k = p['rw_r_k'][i].reshape(RW_H, RW_N)
    y_sum = 0.0
    bonus = 0.0
    finals = []
    for d in range(2):
        w_pre = p['rw_w0'][i, d] + jnp.tanh(pc['rw_wd']) @ p['rw_w2'][i, d]
        decay = jnp.exp(-jnp.exp(-jax.nn.softplus(-w_pre) - 0.5))
        a = jax.nn.sigmoid(p['rw_a0'][i, d] + pc['rw_ad'] @ p['rw_a2'][i, d])
        k = hd(k_base * (1.0 + (a - 1.0) * p['rw_k_a'][i]))
        seq = (r, hd(decay), kk, kk * hd(a), k, v)
        if d == 1:
            seq = tuple(jnp.flip(s, 1) for s in seq)
        y, S = rwkv7_scan(S0[:, d], *seq)
        if d == 1:
            y = jnp.flip(y, 1)
        y_sum = y_sum + y
        bonus = bonus + jnp.sum(r * k * r_k, axis=-1, keepdims=True) * v
        finals.append(S)
    g = jax.nn.sigmoid(pc['rw_gd']) @ p['rw_g2'][i]
    o = (head_norm(y_sum, p['rw_norm_g'][i], center=True) + bonus).reshape(B, L, RW_W) * g
    return o, jnp.stack(finals, axis=1)


def mlstm_chunk_scan(state, q, k, v, ig, lf):
    mask = jnp.tril(jnp.ones((ML_CHUNK, ML_CHUNK), dtype=bool))

    def step(carry, inp):
        C, n, m = carry
        qc, kc, vc, ic, fc = inp
        F = jnp.cumsum(fc, axis=-1)
        Dm = jnp.where(mask, F[..., :, None] - F[..., None, :] + ic[..., None, :], -jnp.inf)
        inter = F + m[..., None]
        mt = jnp.maximum(inter, jnp.max(Dm, axis=-1))
        wi = jnp.exp(Dm - mt[..., None])
        we = jnp.exp(inter - mt)
        s = jnp.einsum('bhtd,bhsd->bhts', qc, kc) * wi
        num = we[..., None] * jnp.einsum('bhvd,bhtd->bhtv', C, qc) + jnp.einsum('bhts,bhsv->bhtv', s, vc)
        den = we * jnp.einsum('bhd,bhtd->bht', n, qc) + jnp.sum(s, axis=-1)
        h = num / jnp.maximum(jnp.abs(den), jnp.exp(-mt))[..., None]
        FL = F[..., -1]
        gs = FL[..., None] - F + ic
        m_new = jnp.maximum(FL + m, jnp.max(gs, axis=-1))
        ws = jnp.exp(gs - m_new[..., None])
        wc = jnp.exp(FL + m - m_new)
        C = wc[..., None, None] * C + jnp.einsum('bhs,bhsv,bhsd->bhvd', ws, vc, kc)
        n = wc[..., None] * n + jnp.einsum('bhs,bhsd->bhd', ws, kc)
        return (C, n, m_new), h

    carry, h = lax.scan(step, state, tuple(to_chunks(a, ML_CHUNK) for a in (q, k, v, ig, lf)))
    return from_chunks(h), carry


def mlstm_branch(pc, state0, i, p):
    C0, n0, m0 = state0
    B, L, _ = pc['ml_q'].shape
    q = pc['ml_q'].reshape(B, L, ML_H, ML_DK)
    k = pc['ml_k'].reshape(B, L, ML_H, ML_DK) * (ML_DK ** -0.5)
    v = pc['ml_v'].reshape(B, L, ML_H, ML_DV)
    ig_all = pc['ml_i'].reshape(B, L, 2, ML_H) + p['ml_i_bias'][i]
    lf_all = jax.nn.log_sigmoid(pc['ml_f'].reshape(B, L, 2, ML_H) + p['ml_f_bias'][i])
    h_sum = 0.0
    Cs, ns, ms = [], [], []
    for d in range(2):
        seq = (q, k, v, ig_all[:, :, d], lf_all[:, :, d])
        if d == 1:
            seq = tuple(jnp.flip(s, 1) for s in seq)
        h, (C, n, m) = mlstm_chunk_scan((C0[:, d], n0[:, d], m0[:, d]), *seq)
        if d == 1:
            h = jnp.flip(h, 1)
        h_sum = h_sum + h
        Cs.append(C)
        ns.append(n)
        ms.append(m)
    o = head_norm(h_sum, p['ml_norm_g'][i]).reshape(B, L, ML_H * ML_DV) * jax.nn.sigmoid(pc['ml_o'])
    return o, (jnp.stack(Cs, axis=1), jnp.stack(ns, axis=1), jnp.stack(ms, axis=1))


def gla_chunk_scan(S0, q, k, v, la):
    mask = jnp.tril(jnp.ones((GL_CHUNK, GL_CHUNK), dtype=bool))

    def step(S, inp):
        qc, kc, vc, lc = inp
        cum = jnp.cumsum(lc, axis=2)
        rel = cum[:, :, :, None, :] - cum[:, :, None, :, :]
        dec = jnp.where(mask[:, :, None], jnp.exp(jnp.minimum(rel, 0.0)), 0.0)
        A = jnp.einsum('bhtd,bhsd,bhtsd->bhts', qc, kc, dec)
        o = jnp.einsum('bhtd,bhdv->bhtv', qc * jnp.exp(cum), S) + jnp.einsum('bhts,bhsv->bhtv', A, vc)
        last = cum[:, :, -1:, :]
        S = jnp.exp(last[:, :, 0, :])[..., None] * S + jnp.einsum('bhsd,bhsv->bhdv', kc * jnp.exp(last - cum), vc)
        return S, o

    S, o = lax.scan(step, S0, tuple(to_chunks(a, GL_CHUNK) for a in (q, k, v, la)))
    return from_chunks(o), S


def gla_branch(pc, S0, i, p):
    B, L, _ = pc['gl_q'].shape
    q = pc['gl_q'].reshape(B, L, GL_H, GL_DK) * (GL_DK ** -0.5)
    k = pc['gl_k'].reshape(B, L, GL_H, GL_DK)
    v = pc['gl_v'].reshape(B, L, GL_H, GL_DV)
    o_sum = 0.0
    finals = []
    for d in range(2):
        la = jax.nn.log_sigmoid(pc['gl_a'] @ p['gl_a2'][i, d] + p['gl_a_bias'][i, d]) / GL_TAU
        seq = (q, k, v, la.reshape(B, L, GL_H, GL_DK))
        if d == 1:
            seq = tuple(jnp.flip(s, 1) for s in seq)
        o, S = gla_chunk_scan(S0[:, d], *seq)
        if d == 1:
            o = jnp.flip(o, 1)
        o_sum = o_sum + o
        finals.append(S)
    out = head_norm(o_sum, p['gl_norm_g'][i]).reshape(B, L, GL_H * GL_DV) * jax.nn.silu(pc['gl_g'])
    return out, jnp.stack(finals, axis=1)


def mixer_block(h, i, p, ctx):
    f32 = jnp.float32
    B, L, _ = h.shape
    u = jnp.einsum('bld,dn->bln', h, p['w_in'][i]).astype(f32)
    u = jnp.concatenate([centred_conv3(u[..., :N_CONV], p['conv_w'][i].astype(f32)), u[..., N_CONV:]], axis=-1)
    pc = dict(zip(IN_NAMES, jnp.split(u, IN_OFFSETS, axis=-1)))
    if ctx is None:
        zeros = lambda *s: jnp.zeros((B, 2) + s, f32)
        rw_S0 = zeros(RW_H, RW_N, RW_N)
        ml_0 = (zeros(ML_H, ML_DV, ML_DK), zeros(ML_H, ML_DK), zeros(ML_H))
        gl_S0 = zeros(GL_H, GL_DK, GL_DV)
    else:
        ctx_k, ctx_v, rw_S0, ml_c0, ml_n0, ml_m0, gl_S0 = (a.astype(f32) for a in ctx)
        ml_0 = (ml_c0, ml_n0, ml_m0)

    q = pc['da_q'].reshape(B, L, DA_H, 2, DA_DK)
    k = pc['da_k'].reshape(B, L, DA_H, 2, DA_DK)
    v = pc['da_v'].reshape(B, L, DA_H, DA_DV)
    if ctx is None:
        keys, vals = k, v
    else:
        q, k_rot = axial_rope_2d(q), axial_rope_2d(k)
        keys = jnp.concatenate([k_rot, ctx_k], axis=1)
        vals = jnp.concatenate([v, ctx_v], axis=1)
    lam_init = 0.8 - 0.6 * math.exp(-0.3 * i)
    lam = (jnp.exp(jnp.sum(p['da_lam_q1'][i].astype(f32) * p['da_lam_k1'][i].astype(f32)))
           - jnp.exp(jnp.sum(p['da_lam_q2'][i].astype(f32) * p['da_lam_k2'][i].astype(f32))) + lam_init)
    o_da = head_norm(diff_attention(q, keys, vals, lam), p['da_norm_g'][i]) * (1.0 - lam_init)
    o_da = o_da.reshape(B, L, DA_H * DA_DV)

    o_rw, rw_S = rwkv_branch(pc, rw_S0, i, p)
    o_ml, (ml_C, ml_n, ml_m) = mlstm_branch(pc, ml_0, i, p)
    o_gl, gl_S = gla_branch(pc, gl_S0, i, p)

    outs = jnp.stack([o_da, o_rw, o_ml, o_gl], axis=2)
    proj = jnp.einsum('blgc,gcd->blgd', outs, p['w_branch'][i])
    gates = jax.nn.sigmoid(jnp.einsum('bld,dn->bln', h, p['w_bgate'][i]).astype(f32))
    merged = jnp.einsum('blgd,blgd->bld', gates.reshape(B, L, N_BRANCH, D_MODEL), proj)
    y = merged @ p['w_out'][i]
    state = (k, v, rw_S, ml_C, ml_n, ml_m, gl_S) if ctx is None else None
    return y, state


def routed_moe(h, i, p):
    B, L, D = h.shape
    x = h.reshape(-1, D)
    T = x.shape[0]
    logits = (x @ p['w_router'][i] + p['b_router'][i]).astype(jnp.float32)
    top_v, top_e = lax.top_k(logits, TOP_K)
    gate = jax.nn.softmax(top_v, axis=-1)
    e_flat = top_e.reshape(-1)
    g_flat = gate.reshape(-1)
    tok_flat = jnp.repeat(jnp.arange(T, dtype=jnp.int32), TOP_K)
    order = jnp.argsort(e_flat)
    e_sorted = e_flat[order]
    counts = jnp.bincount(e_flat, length=N_EXPERTS)
    padded = ((counts + MOE_BLOCK - 1) // MOE_BLOCK) * MOE_BLOCK
    pad_end = jnp.cumsum(padded)
    pad_start = pad_end - padded
    sort_start = jnp.cumsum(counts) - counts
    dest = pad_start[e_sorted] + jnp.arange(T * TOP_K, dtype=jnp.int32) - sort_start[e_sorted]
    n_rows = T * TOP_K + N_EXPERTS * MOE_BLOCK
    n_blocks = n_rows // MOE_BLOCK
    row_tok = jnp.zeros((n_rows,), jnp.int32).at[dest].set(tok_flat[order])
    row_gate = jnp.zeros((n_rows,), jnp.float32).at[dest].set(g_flat[order])
    block_e = jnp.minimum(jnp.searchsorted(pad_end, jnp.arange(n_blocks) * MOE_BLOCK, side='right'), N_EXPERTS - 1)
    wg, bg, wd, bd = p['w_gu'][i], p['b_gu'][i], p['w_dn'][i], p['b_dn'][i]

    def run_block(args):
        tok, g, e = args
        gu = x[tok] @ wg[e] + bg[e]
        glu = jnp.minimum(gu[:, :D_FF], SWIGLU_LIMIT)
        lin = jnp.clip(gu[:, D_FF:], -SWIGLU_LIMIT, SWIGLU_LIMIT)
        act = (lin + 1.0) * glu * jax.nn.sigmoid(SWIGLU_ALPHA * glu)
        return (act @ wd[e] + bd[e]) * g[:, None]

    ys = lax.map(run_block, (row_tok.reshape(n_blocks, MOE_BLOCK), row_gate.reshape(n_blocks, MOE_BLOCK), block_e))
    y = jnp.zeros((T, D), jnp.float32).at[row_tok].add(ys.reshape(n_rows, D).astype(jnp.float32))
    return y.reshape(B, L, D)


def adaln(cvec, i, p):
    m = jax.nn.silu(cvec.astype(jnp.float32)) @ p['w_ada'][i] + p['b_ada'][i]
    return jnp.split(m, 6, axis=-1)


def trunk_layer(x, mod, i, p, ctx):
    sh1, sc1, g1, sh2, sc2, g2 = mod
    h = rms_norm(x, p['norm1_g'][i]) * (1.0 + sc1) + sh1
    y, state = mixer_block(h, i, p, ctx)
    x = (x + g1 * y).astype(x.dtype)
    h = rms_norm(x, p['norm2_g'][i]) * (1.0 + sc2) + sh2
    x = (x + g2 * routed_moe(h, i, p)).astype(x.dtype)
    return x, state


def setup_inputs(seed: int = 0) -> dict:
    key = jax.random.key(seed)
    ks = iter(jax.random.split(key, 64))
    f32 = jnp.float32
    D = D_MODEL

    def nrm(shape, scale=1.0):
        return scale * jax.random.normal(next(ks), shape, f32)

    inp = {}
    inp['x_prompt'] = nrm((BATCH, SEQ, D))
    inp['x_sample'] = nrm((DEC_BATCH, DEC_SEQ, D))
    inp['cache_attn_k'] = nrm((DEC_BATCH, DEPTH, PAST_LEN, DA_H, 2, DA_DK))
    inp['cache_attn_v'] = nrm((DEC_BATCH, DEPTH, PAST_LEN, DA_H, DA_DV))
    inp['state_rwkv'] = nrm((DEC_BATCH, DEPTH, 2, RW_H, RW_N, RW_N), 0.3)
    inp['state_mlstm_c'] = nrm((DEC_BATCH, DEPTH, 2, ML_H, ML_DV, ML_DK), 0.3)
    inp['state_mlstm_n'] = nrm((DEC_BATCH, DEPTH, 2, ML_H, ML_DK), 0.3)
    inp['state_mlstm_m'] = nrm((DEC_BATCH, DEPTH, 2, ML_H))
    inp['state_gla'] = nrm((DEC_BATCH, DEPTH, 2, GL_H, GL_DK, GL_DV), 0.3)
    inp['c'] = nrm((DEC_BATCH, D))
    inp['c_ctx'] = nrm((D,))
    inp['norm1_g'] = 1.0 + nrm((DEPTH, D), 0.1)
    inp['norm2_g'] = 1.0 + nrm((DEPTH, D), 0.1)
    inp['final_g'] = 1.0 + nrm((D,), 0.1)
    inp['w_ada'] = nrm((DEPTH, D, 6 * D), 0.5 * D ** -0.5)
    inp['b_ada'] = nrm((DEPTH, 6 * D), 0.1)
    inp['w_in'] = nrm((DEPTH, D, N_IN), D ** -0.5)
    inp['conv_w'] = jnp.array([0.0, 1.0, 0.0], f32)[None, :, None] + nrm((DEPTH, 3, N_CONV), 0.3)
    inp['da_lam_q1'] = nrm((DEPTH, DA_DK), 0.1)
    inp['da_lam_k1'] = nrm((DEPTH, DA_DK), 0.1)
    inp['da_lam_q2'] = nrm((DEPTH, DA_DK), 0.1)
    inp['da_lam_k2'] = nrm((DEPTH, DA_DK), 0.1)
    inp['da_norm_g'] = 1.0 + nrm((DEPTH, DA_H * DA_DV), 0.1)
    inp['rw_w0'] = nrm((DEPTH, 2, RW_W), 1.0) - 1.5
    inp['rw_w2'] = nrm((DEPTH, 2, RW_DECAY_LORA, RW_W), 0.1)
    inp['rw_a0'] = nrm((DEPTH, 2, RW_W), 0.5)
    inp['rw_a2'] = nrm((DEPTH, 2, RW_A_LORA, RW_W), 0.5 * RW_A_LORA ** -0.5)
    inp['rw_g2'] = nrm((DEPTH, RW_G_LORA, RW_W), RW_G_LORA ** -0.5)
    inp['rw_k_k'] = 0.85 + nrm((DEPTH, RW_W), 0.1)
    inp['rw_k_a'] = 1.0 + nrm((DEPTH, RW_W), 0.1)
    inp['rw_r_k'] = nrm((DEPTH, RW_W), 0.1)
    inp['rw_norm_g'] = 1.0 + nrm((DEPTH, RW_W), 0.1)
    inp['ml_i_bias'] = -2.0 + nrm((DEPTH, 2, ML_H), 0.1)
    inp['ml_f_bias'] = jnp.linspace(3.0, 6.0, ML_H, dtype=f32) + nrm((DEPTH, 2, ML_H), 0.1)
    inp['ml_norm_g'] = 1.0 + nrm((DEPTH, ML_H * ML_DV), 0.1)
    inp['gl_a2'] = nrm((DEPTH, 2, GL_ALPHA_LORA, GL_H * GL_DK), GL_ALPHA_LORA ** -0.5)
    inp['gl_a_bias'] = nrm((DEPTH, 2, GL_H * GL_DK), 0.5)
    inp['gl_norm_g'] = 1.0 + nrm((DEPTH, GL_H * GL_DV), 0.1)
    inp['w_branch'] = nrm((DEPTH, N_BRANCH, BRANCH_W, D), BRANCH_W ** -0.5)
    inp['w_bgate'] = nrm((DEPTH, D, N_BRANCH * D), D ** -0.5)
    inp['w_out'] = nrm((DEPTH, D, D), D ** -0.5)
    inp['w_router'] = nrm((DEPTH, D, N_EXPERTS), D ** -0.5)
    inp['b_router'] = nrm((DEPTH, N_EXPERTS), 0.01)
    inp['w_gu'] = nrm((DEPTH, N_EXPERTS, D, 2 * D_FF), D ** -0.5)
    inp['b_gu'] = nrm((DEPTH, N_EXPERTS, 2 * D_FF), 0.01)
    inp['w_dn'] = nrm((DEPTH, N_EXPERTS, D_FF, D), D_FF ** -0.5)
    inp['b_dn'] = nrm((DEPTH, N_EXPERTS, D), 0.01)
    return inp


def reference(x_prompt, x_sample, cache_attn_k, cache_attn_v, state_rwkv, state_mlstm_c, state_mlstm_n,
              state_mlstm_m, state_gla, c, c_ctx, norm1_g, norm2_g, final_g, w_ada, b_ada, w_in, conv_w,
              da_lam_q1, da_lam_k1, da_lam_q2, da_lam_k2, da_norm_g, rw_w0, rw_w2, rw_a0, rw_a2, rw_g2,
              rw_k_k, rw_k_a, rw_r_k, rw_norm_g, ml_i_bias, ml_f_bias, ml_norm_g, gl_a2, gl_a_bias, gl_norm_g,
              w_branch, w_bgate, w_out, w_router, b_router, w_gu, b_gu, w_dn, b_dn):
    p = dict(norm1_g=norm1_g, norm2_g=norm2_g, w_ada=w_ada, b_ada=b_ada, w_in=w_in, conv_w=conv_w,
             da_lam_q1=da_lam_q1, da_lam_k1=da_lam_k1, da_lam_q2=da_lam_q2, da_lam_k2=da_lam_k2,
             da_norm_g=da_norm_g, rw_w0=rw_w0, rw_w2=rw_w2, rw_a0=rw_a0, rw_a2=rw_a2, rw_g2=rw_g2,
             rw_k_k=rw_k_k, rw_k_a=rw_k_a, rw_r_k=rw_r_k, rw_norm_g=rw_norm_g, ml_i_bias=ml_i_bias,
             ml_f_bias=ml_f_bias, ml_norm_g=ml_norm_g, gl_a2=gl_a2, gl_a_bias=gl_a_bias, gl_norm_g=gl_norm_g,
             w_branch=w_branch, w_bgate=w_bgate, w_out=w_out, w_router=w_router, b_router=b_router,
             w_gu=w_gu, b_gu=b_gu, w_dn=w_dn, b_dn=b_dn)

    x = x_prompt
    per_layer = []
    for i in range(DEPTH):
        x, st = trunk_layer(x, adaln(c_ctx, i, p), i, p, None)
        per_layer.append(st)
    y_prompt = rms_norm(x, final_g)
    new_attn_k = jnp.stack([s[0] for s in per_layer], axis=1)
    new_attn_v = jnp.stack([s[1] for s in per_layer], axis=1)
    new_rwkv = jnp.stack([s[2] for s in per_layer], axis=1)
    new_mlstm_c = jnp.stack([s[3] for s in per_layer], axis=1)
    new_mlstm_n = jnp.stack([s[4] for s in per_layer], axis=1)
    new_mlstm_m = jnp.stack([s[5] for s in per_layer], axis=1)
    new_gla = jnp.stack([s[6] for s in per_layer], axis=1)

    x = x_sample
    for i in range(DEPTH):
        mod = tuple(m[:, None, :] for m in adaln(c, i, p))
        ctx = (cache_attn_k[:, i], cache_attn_v[:, i], state_rwkv[:, i], state_mlstm_c[:, i],
               state_mlstm_n[:, i], state_mlstm_m[:, i], state_gla[:, i])
        x, _ = trunk_layer(x, mod, i, p, ctx)
    y_sample = rms_norm(x, final_g)
    return (y_prompt, y_sample, new_attn_k, new_attn_v, new_rwkv, new_mlstm_c, new_mlstm_n, new_mlstm_m, new_gla)
```

```python
import functools
import math

import numpy as np
import jax
import jax.numpy as jnp
from jax import lax
from jax.experimental import pallas as pl
from jax.experimental.pallas import tpu as pltpu

F32 = jnp.float32
BF16 = jnp.bfloat16

D_MODEL = 1024
GRID_W = 64
N_BRANCH = 4
BRANCH_W = 256
DA_H, DA_DK, DA_DV = 4, 32, 64
RW_H, RW_N = 4, 64
RW_W = RW_H * RW_N
ML_H, ML_DK, ML_DV = 4, 32, 64
GL_H, GL_DK, GL_DV = 4, 32, 64
GL_TAU = 16.0
CHUNK = 64
N_EXPERTS = 32
TOP_K = 4
D_FF = 1024
SWIGLU_ALPHA = 1.702
SWIGLU_LIMIT = 7.0
ROPE_BASE = 10000.0
EPS = 1e-6

LANES = 128
VMEM_LIMIT = 56 * 1024 * 1024

U_RW = 0
U_MLQK = 1024
U_GLQK = 1280
N_CONV = 1536
U_DA = 1536
U_MLVO = 2304
U_GLVG = 2816
U_SMALL = 3328
N_U = 3456

_REF_SPLITS = (('rw_r', 256), ('rw_k', 256), ('rw_v', 256), ('rw_wd', 64), ('rw_ad', 64), ('rw_gd', 128),
               ('ml_q', 128), ('ml_k', 128), ('gl_q', 128), ('gl_k', 128),
               ('da_q', 256), ('da_k', 256), ('da_v', 256), ('ml_v', 256), ('ml_o', 256), ('ml_i', 8),
               ('ml_f', 8), ('gl_v', 256), ('gl_a', 16), ('gl_g', 256))
_REF_OFF = {}
_o = 0
for _n, _s in _REF_SPLITS:
    _REF_OFF[_n] = (_o, _s)
    _o += _s
_MY_ORDER = ('rw_r', 'rw_k', 'rw_v', 'rw_wd', 'rw_ad', 'rw_gd', 'ml_q', 'ml_k', 'gl_q', 'gl_k',
             'da_q', 'da_k', 'da_v', 'ml_v', 'ml_o', 'gl_v', 'gl_g', 'ml_i', 'ml_f', 'gl_a')


def _permute_w_in(w):
    cols = [w[:, _REF_OFF[n][0]:_REF_OFF[n][0] + _REF_OFF[n][1]] for n in _MY_ORDER]
    cols.append(jnp.zeros((w.shape[0], N_U - sum(_REF_OFF[n][1] for n in _MY_ORDER)), w.dtype))
    return jnp.concatenate(cols, axis=1)


def _cparams(sem):
    return pltpu.CompilerParams(dimension_semantics=sem, vmem_limit_bytes=VMEM_LIMIT)


def _sigmoid(x):
    return 1.0 / (1.0 + jnp.exp(-x))


def _log_sigmoid(x):
    return jnp.minimum(x, 0.0) - jnp.log(1.0 + jnp.exp(-jnp.abs(x)))


def _dot(a, b, prec=None):
    return jnp.dot(a, b, preferred_element_type=F32, precision=prec)


def _dot_nt(a, b, prec=None):
    return lax.dot_general(a, b, (((1,), (1,)), ((), ())), preferred_element_type=F32, precision=prec)


def _dot_tn(a, b, prec=None):
    return lax.dot_general(a, b, (((0,), (0,)), ((), ())), preferred_element_type=F32, precision=prec)


HI = lax.Precision.HIGHEST


def _adaln_kernel(c_ref, w_ref, b_ref, o_ref):
    c = c_ref[...]
    s = c * _sigmoid(c)
    o_ref[0] = _dot(s, w_ref[0], HI) + b_ref[0]


def adaln_all(cvecs, w_ada, b_ada):
    depth, d, n = w_ada.shape
    r = cvecs.shape[0]
    tn = 1536
    return pl.pallas_call(
        _adaln_kernel,
        grid=(depth, n // tn),
        in_specs=[pl.BlockSpec((r, d), lambda l, j: (0, 0)),
                  pl.BlockSpec((1, d, tn), lambda l, j: (l, 0, j)),
                  pl.BlockSpec((1, 1, tn), lambda l, j: (l, 0, j))],
        out_specs=pl.BlockSpec((1, r, tn), lambda l, j: (l, 0, j)),
        out_shape=jax.ShapeDtypeStruct((depth, r, n), F32),
        compiler_params=_cparams(("arbitrary", "arbitrary")),
    )(cvecs, w_ada, b_ada.reshape(depth, 1, n))


def _rms(x, g):
    return x * lax.rsqrt(jnp.mean(x * x, axis=-1, keepdims=True) + EPS) * g


def _inproj_kernel(x_ref, mod_ref, g_ref, w_ref, u_ref, h_ref):
    x = x_ref[...]
    mod = mod_ref[0]
    h = _rms(x, g_ref[...]) * (1.0 + mod[1:2]) + mod[0:1]
    hb = h.astype(BF16)
    h_ref[...] = hb
    u_ref[...] = _dot(hb, w_ref[...])


def in_projection(x, mod, mod_index, norm_g, w_in_b, tm):
    t, d = x.shape
    return pl.pallas_call(
        _inproj_kernel,
        grid=(t // tm,),
        in_specs=[pl.BlockSpec((tm, d), lambda i: (i, 0)),
                  pl.BlockSpec((1, 8, d), lambda i: (mod_index(i), 0, 0)),
                  pl.BlockSpec((1, d), lambda i: (0, 0)),
                  pl.BlockSpec((d, N_U), lambda i: (0, 0))],
        out_specs=[pl.BlockSpec((tm, N_U), lambda i: (i, 0)),
                   pl.BlockSpec((tm, d), lambda i: (i, 0))],
        out_shape=[jax.ShapeDtypeStruct((t, N_U), F32), jax.ShapeDtypeStruct((t, d), BF16)],
        compiler_params=_cparams(("parallel",)),
    )(x, mod, norm_g.reshape(1, d), w_in_b)


ROUTE_W = LANES


def _to_tiles(o_ref, y):
    for s in range(y.shape[1] // LANES):
        o_ref[:, s, :] = y[:, s * LANES:(s + 1) * LANES]


def _from_tiles(x3):
    return jnp.concatenate([x3[:, s, :] for s in range(x3.shape[1])], axis=1)


def _merge_kernel(x_ref, h_ref, oda_c, oda_d, orw_c, orw_d, oml_c, oml_d, ogl_c, ogl_d, mod_ref, wbr_ref, wbg_ref,
                  wout_ref, g2_ref, wr_ref, br_ref, x1_ref, h2_ref, route_ref, cnt_ref, *, n_ctx_tiles):
    d = x_ref.shape[1]
    h = h_ref[...]
    mod = mod_ref[0]
    is_ctx = pl.program_id(0) < n_ctx_tiles
    merged = jnp.zeros(x_ref.shape, F32)
    for gi, (oc_ref, od_ref) in enumerate(((oda_c, oda_d), (orw_c, orw_d), (oml_c, oml_d), (ogl_c, ogl_d))):
        gate = _sigmoid(_dot(h, wbg_ref[:, gi * d:(gi + 1) * d]))
        o = jnp.where(is_ctx, oc_ref[...], od_ref[...])
        merged = merged + gate * _dot(o, wbr_ref[gi])
    y = _dot(merged.astype(BF16), wout_ref[...])
    x1 = x_ref[...] + mod[2:3] * y
    x1_ref[...] = x1
    h2 = _rms(x1, g2_ref[...]) * (1.0 + mod[4:5]) + mod[3:4]
    _to_tiles(h2_ref, h2)
    lane = lax.broadcasted_iota(jnp.int32, (1, ROUTE_W), 1)
    logits = jnp.where(lane < N_EXPERTS, _dot(h2.astype(BF16), wr_ref[...]) + br_ref[...], -jnp.inf)
    vals, ids = [], []
    for _ in range(TOP_K):
        m = jnp.max(logits, axis=-1, keepdims=True)
        idx = jnp.min(jnp.where(logits == m, lane, ROUTE_W), axis=-1, keepdims=True)
        vals.append(m)
        ids.append(idx)
        logits = jnp.where(lane == idx, -jnp.inf, logits)
    es = [jnp.exp(v - vals[0]) for v in vals]
    inv = 1.0 / sum(es)
    route = jnp.zeros((x_ref.shape[0], ROUTE_W), F32)
    for r in range(TOP_K):
        route = jnp.where(lane == r, es[r] * inv, route)
        route = jnp.where(lane == TOP_K + r, ids[r].astype(F32), route)
    route_ref[...] = route
    hist = sum(jnp.sum((lane == ids[r]).astype(F32), axis=0, keepdims=True) for r in range(TOP_K))
    cnt_ref[0] = jnp.broadcast_to(hist, cnt_ref.shape[1:])


def merge_and_route(x, h, outs_ctx, outs_dec, mod, mod_index, w_branch_b, w_bgate_b, w_out_b, norm2_g, w_router,
                    b_router, tm):
    t, d = x.shape
    nct = outs_ctx[0].shape[0] // tm
    ctx_row = pl.BlockSpec((tm, BRANCH_W), lambda i: (jnp.minimum(i, nct - 1), 0))
    dec_row = pl.BlockSpec((tm, BRANCH_W), lambda i: (jnp.maximum(i - nct, 0), 0))
    wr = jnp.zeros((d, ROUTE_W), BF16).at[:, :N_EXPERTS].set(w_router.astype(BF16))
    br = jnp.zeros((1, ROUTE_W), F32).at[0, :N_EXPERTS].set(b_router)
    row = lambda w: pl.BlockSpec((tm, w), lambda i: (i, 0))
    full = lambda *shape: pl.BlockSpec(shape, lambda i: (0,) * len(shape))
    return pl.pallas_call(
        functools.partial(_merge_kernel, n_ctx_tiles=nct),
        grid=(t // tm,),
        in_specs=[row(d), row(d)] + [ctx_row, dec_row] * N_BRANCH + [
                  pl.BlockSpec((1, 8, d), lambda i: (mod_index(i), 0, 0)),
                  full(N_BRANCH, BRANCH_W, d), full(d, N_BRANCH * d), full(d, d), full(1, d),
                  full(d, ROUTE_W), full(1, ROUTE_W)],
        out_specs=[row(d), pl.BlockSpec((tm, d // LANES, LANES), lambda i: (i, 0, 0)), row(ROUTE_W),
                   pl.BlockSpec((1, 8, ROUTE_W), lambda i: (i, 0, 0))],
        out_shape=[jax.ShapeDtypeStruct((t, d), F32), jax.ShapeDtypeStruct((t, d // LANES, LANES), F32),
                   jax.ShapeDtypeStruct((t, ROUTE_W), F32), jax.ShapeDtypeStruct((t // tm, 8, ROUTE_W), F32)],
        compiler_params=_cparams(("parallel",)),
    )(x, h, *[o for pair in zip(outs_ctx, outs_dec) for o in pair], mod, w_branch_b, w_bgate_b, w_out_b,
      norm2_g.reshape(1, d), wr, br)


MOE_BLOCK = 256
ISSUE_UNROLL = 8
assert TOP_K == 4


def _moe_kernel(be_ref, slot_ref, nact_ref, h_hbm, gate_ref, wg_ref, bg_ref, wd_ref, bd_ref, y_hbm,
                xbuf, ybuf, wg_b, wd_b, sem_in, sem_out):
    i = pl.program_id(0)
    n_act = nact_ref[0]
    blk = MOE_BLOCK
    n_tok = h_hbm.shape[0]
    cur = i % 2

    def start_gather(block, buf_slot):
        def issue(r, carry):
            tok = jnp.minimum(lax.shift_right_logical(slot_ref[block * blk + r], 2), n_tok - 1)
            pltpu.make_async_copy(h_hbm.at[tok], xbuf.at[buf_slot, r], sem_in.at[buf_slot]).start()
            return carry
        lax.fori_loop(0, blk, issue, 0, unroll=ISSUE_UNROLL)

    def wait_gather(buf_slot):
        pltpu.make_async_copy(h_hbm.at[pl.ds(0, blk)], xbuf.at[buf_slot], sem_in.at[buf_slot]).wait()

    def start_scatter(block, buf_slot):
        def issue(r, carry):
            pltpu.make_async_copy(ybuf.at[buf_slot, r], y_hbm.at[slot_ref[block * blk + r]],
                                  sem_out.at[buf_slot]).start()
            return carry
        lax.fori_loop(0, blk, issue, 0, unroll=ISSUE_UNROLL)

    def wait_scatter(buf_slot):
        pltpu.make_async_copy(ybuf.at[buf_slot], y_hbm.at[pl.ds(0, blk)], sem_out.at[buf_slot]).wait()

    @pl.when(i == 0)
    def _():
        n_slots = y_hbm.shape[0] - 2 * blk
        ybuf[0] = jnp.zeros(ybuf.shape[1:], F32)
        for bank in range(2):
            fill = pltpu.make_async_copy(ybuf.at[0], y_hbm.at[pl.ds(n_slots + bank * blk, blk)], sem_out.at[0])
            fill.start()
            fill.wait()

    @pl.when((i == 0) & (n_act > 0))
    def _():
        start_gather(0, 0)

    @pl.when(i < n_act)
    def _():
        @pl.when(i + 1 < n_act)
        def _():
            start_gather(i + 1, 1 - cur)

        @pl.when((i == 0) | (be_ref[i] != be_ref[jnp.maximum(i - 1, 0)]))
        def _():
            rows = 128

            def cast(c, carry):
                sl = pl.ds(pl.multiple_of(c * rows, rows), rows)
                wg_b[sl, :] = wg_ref[0, sl, :].astype(BF16)
                wd_b[sl, :] = wd_ref[0, sl, :].astype(BF16)
                return carry

            lax.fori_loop(0, wg_b.shape[0] // rows, cast, 0)

        wait_gather(cur)
        x = _from_tiles(xbuf.at[cur]).astype(BF16)
        gu = _dot(x, wg_b[...]) + bg_ref[0]
        glu = jnp.minimum(gu[:, :D_FF], SWIGLU_LIMIT)
        lin = jnp.clip(gu[:, D_FF:], -SWIGLU_LIMIT, SWIGLU_LIMIT)
        act = (lin + 1.0) * glu * _sigmoid(SWIGLU_ALPHA * glu)
        y = (_dot(act.astype(BF16), wd_b[...]) + bd_ref[0]) * gate_ref[...]

        @pl.when(i >= 2)
        def _():
            wait_scatter(cur)

        _to_tiles(ybuf.at[cur], y)
        start_scatter(i, cur)

    @pl.when(i == pl.num_programs(0) - 1)
    def _():
        @pl.when(n_act >= 1)
        def _():
            wait_scatter((n_act - 1) % 2)

        @pl.when(n_act >= 2)
        def _():
            wait_scatter(n_act % 2)


def moe_experts(h3, block_e, row_slot, n_act, row_gate, w_gu, b_gu, w_dn, b_dn, n_slots):
    n_rows = row_slot.shape[0]
    n_blocks = n_rows // MOE_BLOCK
    e, d, f2 = w_gu.shape
    s = d // LANES
    grid_spec = pltpu.PrefetchScalarGridSpec(
        num_scalar_prefetch=3,
        grid=(n_blocks,),
        in_specs=[pl.BlockSpec(memory_space=pl.ANY),
                  pl.BlockSpec((MOE_BLOCK, 1), lambda i, be, sl, na: (i, 0)),
                  pl.BlockSpec((1, d, f2), lambda i, be, sl, na: (be[i], 0, 0)),
                  pl.BlockSpec((1, 1, f2), lambda i, be, sl, na: (be[i], 0, 0)),
                  pl.BlockSpec((1, f2 // 2, d), lambda i, be, sl, na: (be[i], 0, 0)),
                  pl.BlockSpec((1, 1, d), lambda i, be, sl, na: (be[i], 0, 0))],
        out_specs=pl.BlockSpec(memory_space=pl.ANY),
        scratch_shapes=[pltpu.VMEM((2, MOE_BLOCK, s, LANES), F32), pltpu.VMEM((2, MOE_BLOCK, s, LANES), F32),
                        pltpu.VMEM((d, f2), BF16), pltpu.VMEM((f2 // 2, d), BF16),
                        pltpu.SemaphoreType.DMA((2,)), pltpu.SemaphoreType.DMA((2,))],
    )
    return pl.pallas_call(
        _moe_kernel,
        grid_spec=grid_spec,
        out_shape=jax.ShapeDtypeStruct((n_slots + 2 * MOE_BLOCK, s, LANES), F32),
        compiler_params=_cparams(("arbitrary",)),
    )(block_e, row_slot, n_act, h3, row_gate.reshape(n_rows, 1), w_gu, b_gu.reshape(e, 1, f2),
      w_dn, b_dn.reshape(e, 1, d))


def _combine_kernel(y_ref, x1_ref, mod_ref, fg_ref, o_ref, *, final):
    tc = x1_ref.shape[0]
    y = jnp.concatenate(
        [sum(y_ref[pl.ds(j, tc, stride=TOP_K), s, :] for j in range(TOP_K)) for s in range(y_ref.shape[1])],
        axis=1)
    x2 = x1_ref[...] + mod_ref[0][5:6] * y
    if final:
        x2 = _rms(x2, fg_ref[...])
    o_ref[...] = x2


def moe_combine(ys3, x1, mod, mod_index, final_g, tc, final):
    t, d = x1.shape
    s = d // LANES
    return pl.pallas_call(
        functools.partial(_combine_kernel, final=final),
        grid=(t // tc,),
        in_specs=[pl.BlockSpec((tc * TOP_K, s, LANES), lambda i: (i, 0, 0)),
                  pl.BlockSpec((tc, d), lambda i: (i, 0)),
                  pl.BlockSpec((1, 8, d), lambda i: (mod_index(i), 0, 0)),
                  pl.BlockSpec((1, d), lambda i: (0, 0))],
        out_specs=pl.BlockSpec((tc, d), lambda i: (i, 0)),
        out_shape=jax.ShapeDtypeStruct((t, d), F32),
        compiler_params=_cparams(("parallel",)),
    )(ys3, x1, mod, final_g.reshape(1, d))


def _route_metadata(route, tile_counts, n_tok):
    gates = route[:, 0:TOP_K].reshape(-1)
    experts = route[:, TOP_K:2 * TOP_K].astype(jnp.int32).reshape(-1)
    n_slots = n_tok * TOP_K
    n_rows = n_slots + N_EXPERTS * MOE_BLOCK
    n_blocks = n_rows // MOE_BLOCK
    order = jnp.argsort(experts, stable=True).astype(jnp.int32)
    counts = jnp.sum(tile_counts[:, 0, :N_EXPERTS], axis=0).astype(jnp.int32)
    sort_start = jnp.cumsum(counts) - counts
    padded = ((counts + MOE_BLOCK - 1) // MOE_BLOCK) * MOE_BLOCK
    pad_end = jnp.cumsum(padded)
    pad_start = pad_end - padded
    block_start = jnp.arange(n_blocks, dtype=jnp.int32) * MOE_BLOCK
    block_e = jnp.minimum(jnp.sum((pad_end[None, :] <= block_start[:, None]).astype(jnp.int32), axis=1),
                          N_EXPERTS - 1)
    row = jnp.arange(n_rows, dtype=jnp.int32)
    per_row = lambda table: jnp.repeat(table[block_e], MOE_BLOCK)
    rank = row - per_row(pad_start)
    valid = rank < per_row(counts)
    src = order[jnp.clip(per_row(sort_start) + rank, 0, n_slots - 1)]
    scratch_row = n_slots + ((row // MOE_BLOCK) % 2) * MOE_BLOCK + row % MOE_BLOCK
    row_slot = jnp.where(valid, src, scratch_row).astype(jnp.int32)
    row_gate = jnp.where(valid, gates[src], 0.0)
    n_act = (pad_end[-1] // MOE_BLOCK).astype(jnp.int32).reshape(1)
    return block_e, row_slot, n_act, row_gate


def _split3(x):
    hi = x.astype(BF16)
    r1 = x - hi.astype(F32)
    mid = r1.astype(BF16)
    lo = (r1 - mid.astype(F32)).astype(BF16)
    return hi, mid, lo


def _dot_exact_rhs(x, c_b):
    hi, mid, lo = _split3(x)
    return _dot(hi, c_b) + _dot(mid, c_b) + _dot(lo, c_b)


def _dot_exact_lhs(c_b, x):
    hi, mid, lo = _split3(x)
    return _dot(c_b, hi) + _dot(c_b, mid) + _dot(c_b, lo)


def _group_matrix(n, group, value):
    r = lax.broadcasted_iota(jnp.int32, (n, n), 0) // group
    c = lax.broadcasted_iota(jnp.int32, (n, n), 1) // group
    return jnp.where(r == c, value, 0.0).astype(BF16)


def _head_rms(x, gmat, g):
    ms = _dot_exact_rhs(x * x, gmat)
    return x * lax.rsqrt(ms + EPS) * g


def _conv3(u, w, first, last):
    n = u.shape[0]
    prev = jnp.where(first, 0.0, pltpu.roll(u, 1, 0))
    nxt = jnp.where(last, 0.0, pltpu.roll(u, n - 1, 0))
    return prev * w[0:1] + u * w[1:2] + nxt * w[2:3]


def _rope_tables(n_tok):
    half = DA_DK // 4
    freqs = (ROPE_BASE ** (-np.arange(half, dtype=np.float32) / half)).astype(np.float32)
    t = np.arange(n_tok)
    row = (t // GRID_W).astype(np.float32)
    col = (t % GRID_W).astype(np.float32)
    cos32 = np.zeros((n_tok, DA_DK), np.float64)
    sin32 = np.zeros((n_tok, DA_DK), np.float64)
    for base, pos in ((0, row), (2 * half, col)):
        ang = (pos[:, None] * freqs[None, :]).astype(np.float32).astype(np.float64)
        cos32[:, base:base + half] = np.cos(ang)
        cos32[:, base + half:base + 2 * half] = np.cos(ang)
        sin32[:, base:base + half] = -np.sin(ang)
        sin32[:, base + half:base + 2 * half] = np.sin(ang)
    reps = DA_H * 2
    return (jnp.asarray(np.tile(cos32, (1, reps)), F32), jnp.asarray(np.tile(sin32, (1, reps)), F32))


def _rope(x, cos, sin):
    n = x.shape[1]
    lane = lax.broadcasted_iota(jnp.int32, x.shape, 1)
    partner = jnp.where((lane % 16) < 8, pltpu.roll(x, n - 8, 1), pltpu.roll(x, 8, 1))
    return x * cos + partner * sin


def _attn_core(q, kall, vall, lam, g, o_ref, lam_init):
    tq = q.shape[0]
    lane = lax.broadcasted_iota(jnp.int32, (1, 256), 1)
    acc = jnp.zeros((tq, 256), F32)
    for h in range(DA_H):
        probs = []
        for m in range(2):
            c0 = h * 2 * DA_DK + m * DA_DK
            qm = jnp.where((lane >= c0) & (lane < c0 + DA_DK), q, 0.0).astype(BF16)
            s = _dot_nt(qm, kall)
            s = s - jnp.max(s, axis=-1, keepdims=True)
            p = jnp.exp(s)
            probs.append(p / jnp.sum(p, axis=-1, keepdims=True))
        a = (probs[0] - lam * probs[1]).astype(BF16)
        vh = jnp.where((lane >= h * DA_DV) & (lane < (h + 1) * DA_DV), vall, jnp.zeros_like(vall))
        acc = acc + _dot(a, vh)
    gmat = _group_matrix(256, DA_DV, 1.0 / DA_DV)
    o_ref[...] = (_head_rms(acc, gmat, g) * (1.0 - lam_init)).astype(o_ref.dtype)


def _lam(lp):
    return (jnp.exp(jnp.sum(lp[0:1] * lp[1:2], axis=-1, keepdims=True))
            - jnp.exp(jnp.sum(lp[2:3] * lp[3:4], axis=-1, keepdims=True)))


def _attn_ctx_kernel(lp_ref, q_ref, k_ref, v_ref, g_ref, o_ref, *, lam_init):
    lam = _lam(lp_ref[...]) + lam_init
    q = q_ref[...] * (DA_DK ** -0.5)
    _attn_core(q, k_ref[...].astype(BF16), v_ref[...].astype(BF16), lam, g_ref[...], o_ref, lam_init)


def _attn_dec_kernel(lp_ref, q_ref, k_ref, v_ref, ck_ref, cv_ref, cosq_ref, sinq_ref, cosk_ref, sink_ref,
                     g_ref, o_ref, kall, vall, *, lam_init):
    n_lat = k_ref.shape[0]
    n_past = ck_ref.shape[1]

    @pl.when(pl.program_id(1) == 0)
    def _():
        kall[0:n_lat, :] = _rope(k_ref[...], cosk_ref[...], sink_ref[...]).astype(BF16)
        kall[n_lat:n_lat + n_past, :] = ck_ref[0].astype(BF16)
        vall[0:n_lat, :] = v_ref[...].astype(BF16)
        vall[n_lat:n_lat + n_past, :] = cv_ref[0].astype(BF16)

    lam = _lam(lp_ref[...]) + lam_init
    q = _rope(q_ref[...], cosq_ref[...], sinq_ref[...]) * (DA_DK ** -0.5)
    _attn_core(q, kall[...], vall[...], lam, g_ref[...], o_ref, lam_init)


def attention_ctx(u, lam_p, norm_g, layer, n_b, seq, row0):
    lam_init = 0.8 - 0.6 * math.exp(-0.3 * layer)
    rb = row0 // seq
    cq, ck, cv = U_DA // 256, U_DA // 256 + 1, U_DA // 256 + 2
    return pl.pallas_call(
        functools.partial(_attn_ctx_kernel, lam_init=lam_init),
        grid=(n_b,),
        in_specs=[pl.BlockSpec((4, DA_DK), lambda b: (0, 0)),
                  pl.BlockSpec((seq, 256), lambda b: (rb + b, cq)),
                  pl.BlockSpec((seq, 256), lambda b: (rb + b, ck)),
                  pl.BlockSpec((seq, 256), lambda b: (rb + b, cv)),
                  pl.BlockSpec((1, 256), lambda b: (0, 0))],
        out_specs=pl.BlockSpec((seq, 256), lambda b: (b, 0)),
        out_shape=jax.ShapeDtypeStruct((n_b * seq, 256), BF16),
        compiler_params=_cparams(("parallel",)),
    )(lam_p, u, u, u, norm_g.reshape(1, 256))


def attention_dec(u, cache_k, cache_v, lam_p, norm_g, layer, n_b, seq, row0, tq=256):
    lam_init = 0.8 - 0.6 * math.exp(-0.3 * layer)
    rb = row0 // seq
    rq = row0 // tq
    nq = seq // tq
    n_past = cache_k.shape[1]
    cq, ck, cv = U_DA // 256, U_DA // 256 + 1, U_DA // 256 + 2
    cos, sin = _rope_tables(seq)
    return pl.pallas_call(
        functools.partial(_attn_dec_kernel, lam_init=lam_init),
        grid=(n_b, nq),
        in_specs=[pl.BlockSpec((4, DA_DK), lambda b, i: (0, 0)),
                  pl.BlockSpec((tq, 256), lambda b, i: (rq + b * nq + i, cq)),
                  pl.BlockSpec((seq, 256), lambda b, i: (rb + b, ck)),
                  pl.BlockSpec((seq, 256), lambda b, i: (rb + b, cv)),
                  pl.BlockSpec((1, n_past, 256), lambda b, i: (b, 0, 0)),
                  pl.BlockSpec((1, n_past, 256), lambda b, i: (b, 0, 0)),
                  pl.BlockSpec((tq, 256), lambda b, i: (i, 0)),
                  pl.BlockSpec((tq, 256), lambda b, i: (i, 0)),
                  pl.BlockSpec((seq, 256), lambda b, i: (0, 0)),
                  pl.BlockSpec((seq, 256), lambda b, i: (0, 0)),
                  pl.BlockSpec((1, 256), lambda b, i: (0, 0))],
        out_specs=pl.BlockSpec((tq, 256), lambda b, i: (b * nq + i, 0)),
        out_shape=jax.ShapeDtypeStruct((n_b * seq, 256), BF16),
        scratch_shapes=[pltpu.VMEM((seq + n_past, 256), BF16), pltpu.VMEM((seq + n_past, 256), BF16)],
        compiler_params=_cparams(("parallel", "arbitrary")),
    )(lam_p, u, u, u, cache_k, cache_v, cos, sin, cos, sin, norm_g.reshape(1, 256))


def _conv_chunk(u_ref, cw, c, n_chunks, col0, ncol, base=0):
    end = base + n_chunks * CHUNK
    r0 = pl.multiple_of(base + c * CHUNK, CHUNK)
    x = u_ref[pl.ds(r0, CHUNK), col0:col0 + ncol]
    hp = u_ref[pl.ds(pl.multiple_of(jnp.maximum(r0 - 8, base), 8), 8), col0:col0 + ncol][7:8]
    hn = u_ref[pl.ds(pl.multiple_of(jnp.minimum(r0 + CHUNK, end - 8), 8), 8), col0:col0 + ncol][0:1]
    hp = jnp.where(c == 0, 0.0, hp)
    hn = jnp.where(c == n_chunks - 1, 0.0, hn)
    row = lax.broadcasted_iota(jnp.int32, (CHUNK, 1), 0)
    prev = jnp.where(row == 0, hp, pltpu.roll(x, 1, 0))
    nxt = jnp.where(row == CHUNK - 1, hn, pltpu.roll(x, CHUNK - 1, 0))
    return prev * cw[0:1] + x * cw[1:2] + nxt * cw[2:3]


def _cumsum_rows(x):
    row = lax.broadcasted_iota(jnp.int32, (x.shape[0], 1), 0)
    s = 1
    while s < x.shape[0]:
        x = x + jnp.where(row >= s, pltpu.roll(x, s, 0), 0.0)
        s *= 2
    return x


def _time_cumsum(x, reverse):
    p = _cumsum_rows(x)
    total = p[x.shape[0] - 1:x.shape[0]]
    if reverse:
        p = total - p + x
    return p, total


def _before_masks(reverse):
    t = lax.broadcasted_iota(jnp.int32, (CHUNK, CHUNK), 0)
    s = lax.broadcasted_iota(jnp.int32, (CHUNK, CHUNK), 1)
    if reverse:
        return s > t, s >= t
    return s < t, s <= t


def _mm(a, b):
    return _dot(a.astype(BF16), b.astype(BF16))


def _mm_nt(a, b):
    return _dot_nt(a.astype(BF16), b.astype(BF16))


def _mm_tn(a, b):
    return _dot_tn(a.astype(BF16), b.astype(BF16))


def _rwkv_chunk(uc, d, prm, s_ref):
    w0, w2, a0, a2, k_k, k_a, r_k = prm
    reverse = d == 1
    r = uc[:, 0:256]
    kb = uc[:, 256:512]
    v = uc[:, 512:768]
    wd = uc[:, 768:832]
    ad = uc[:, 832:896]
    ones_g = _group_matrix(RW_W, RW_N, 1.0)
    kk = kb * k_k
    kk = kk * lax.rsqrt(_dot_exact_rhs(kk * kk, ones_g) + EPS)
    w_pre = w0[d:d + 1] + _dot(jnp.tanh(wd), w2[d], HI)
    softplus_neg = jnp.maximum(-w_pre, 0.0) + jnp.log(1.0 + jnp.exp(-jnp.abs(w_pre)))
    logw = -jnp.exp(-softplus_neg - 0.5)
    a = _sigmoid(a0[d:d + 1] + _dot(ad, a2[d], HI))
    k = kb * (1.0 + (a - 1.0) * k_a)
    b = kk * a
    cum, total = _time_cumsum(logw, reverse)
    e_in = jnp.exp(cum)
    e_out = jnp.exp(-cum)
    kap = kk * jnp.exp(cum - logw)
    bet = b * e_out
    kt = k * e_out
    rt = r * e_in
    g_end = jnp.exp(total)
    n = RW_H * CHUNK
    row = lax.broadcasted_iota(jnp.int32, (n, n), 0)
    col = lax.broadcasted_iota(jnp.int32, (n, n), 1)
    same_head = (row // CHUNK) == (col // RW_N)
    t_row, t_col = row % CHUNK, col % CHUNK
    strict = (t_col > t_row) if reverse else (t_col < t_row)
    incl = (t_col >= t_row) if reverse else (t_col <= t_row)
    eye = (row == col).astype(F32)

    def bd(x):
        xb = x.astype(BF16)
        return jnp.where(same_head, jnp.concatenate([xb] * RW_H, axis=0), jnp.zeros((n, n), BF16))

    kap_b, rt_b, bet_b, kt_b, v_b = bd(kap), bd(rt), bd(bet), bd(kt), bd(v)
    x = jnp.concatenate([kap_b, rt_b], axis=0)
    gb = _dot_nt(x, bet_b)
    gk = _dot_nt(x, kt_b)
    a_b = jnp.where(strict, gb[:n], 0.0)
    m_b = jnp.where(incl, gb[n:], 0.0)
    a_k = jnp.where(strict, gk[:n], 0.0)
    m_k = jnp.where(incl, gk[n:], 0.0)
    npow = -a_b
    tinv = eye + npow
    for _ in range(5):
        npow = _mm(npow, npow)
        tinv = tinv + _mm(tinv, npow)
    s0 = s_ref[...]
    xs = _dot_nt(x, s0.astype(BF16))
    u = -_mm(tinv, xs[:n] + _dot(a_k.astype(BF16), v_b))
    uv = jnp.concatenate([u.astype(BF16), v_b], axis=0)
    y_bd = xs[n:] + _dot(jnp.concatenate([m_b, m_k], axis=1).astype(BF16), uv)
    s_ref[...] = (s0 + _dot_tn(uv, jnp.concatenate([bet_b, kt_b], axis=0))) * g_end
    y = y_bd[0:CHUNK]
    for h in range(1, RW_H):
        y = y + y_bd[h * CHUNK:(h + 1) * CHUNK]
    bonus = _dot_exact_rhs(r * k * r_k, ones_g) * v
    return y, bonus


def _rwkv_kernel(*refs, has_state):
    if has_state:
        (u_ref, cw_ref, w0_ref, w2_ref, a0_ref, a2_ref, g2_ref, vec_ref, s0_ref, o_ref, sf_ref) = refs[:11]
        scratch = refs[11:]
    else:
        (u_ref, cw_ref, w0_ref, w2_ref, a0_ref, a2_ref, g2_ref, vec_ref, o_ref, sf_ref) = refs[:10]
        scratch = refs[10:]
    s_refs, (y_refs, bonus_refs) = scratch[:-4], (scratch[-4:-2], scratch[-2:])
    n_seq = sf_ref.shape[0]
    seq = u_ref.shape[0] // n_seq
    n_chunks = seq // CHUNK
    for g in range(n_seq):
        for d in range(2):
            s_ref = s_refs[2 * g + d]
            s_ref[...] = jnp.zeros(s_ref.shape, F32)
            if has_state:
                for h in range(RW_H):
                    s_ref[h * RW_N:(h + 1) * RW_N, h * RW_N:(h + 1) * RW_N] = s0_ref[g, d, h]
    vec = vec_ref[...]
    prm = (w0_ref[...], w2_ref, a0_ref[...], a2_ref, vec[0:1], vec[1:2], vec[2:3])
    cw = cw_ref[...]

    def step(j, carry):
        for g in range(n_seq):
            for d in range(2):
                c = j if d == 0 else n_chunks - 1 - j
                uc = _conv_chunk(u_ref, cw, c, n_chunks, 0, 1024, base=g * seq)
                y, bonus = _rwkv_chunk(uc, d, prm, s_refs[2 * g + d])
                rows = pl.ds(pl.multiple_of(g * seq + c * CHUNK, CHUNK), CHUNK)
                y_refs[d][rows, :] = y
                bonus_refs[d][rows, :] = bonus
        return carry

    lax.fori_loop(0, n_chunks, step, 0)
    for g in range(n_seq):
        for d in range(2):
            for h in range(RW_H):
                sf_ref[g, d, h] = s_refs[2 * g + d][h * RW_N:(h + 1) * RW_N, h * RW_N:(h + 1) * RW_N]

    mean_g = _group_matrix(RW_W, RW_N, 1.0 / RW_N)

    def finish(c, carry):
        for g in range(n_seq):
            rows = pl.ds(pl.multiple_of(g * seq + c * CHUNK, CHUNK), CHUNK)
            gd = _conv_chunk(u_ref, cw[:, 896:1024], c, n_chunks, 896, 128, base=g * seq)
            gate = _mm(_sigmoid(gd), g2_ref[...])
            y = y_refs[0][rows, :] + y_refs[1][rows, :]
            bonus = bonus_refs[0][rows, :] + bonus_refs[1][rows, :]
            yc = y - _dot_exact_rhs(y, mean_g)
            o_ref[rows, :] = ((_head_rms(yc, mean_g, vec[3:4]) + bonus) * gate).astype(o_ref.dtype)
        return carry

    lax.fori_loop(0, n_chunks, finish, 0)


def _transpose_small(x):
    eye = (lax.broadcasted_iota(jnp.int32, (CHUNK, CHUNK), 0)
           == lax.broadcasted_iota(jnp.int32, (CHUNK, CHUNK), 1)).astype(BF16)
    hi, mid, lo = _split3(x)
    return _dot_tn(hi, eye) + _dot_tn(mid, eye) + _dot_tn(lo, eye)


def _mlstm_chunk(qk, v, small, d, bias, c_ref, n_ref, m_ref):
    reverse = d == 1
    lane = lax.broadcasted_iota(jnp.int32, (1, LANES), 1)
    pre = small + bias
    lf = jnp.where((lane >= 8) & (lane < 16), _log_sigmoid(pre), 0.0)
    fcum, ftot = _time_cumsum(lf, reverse)
    w = jnp.where((lane >= 8) & (lane < 16), fcum, pre)
    wt = _transpose_small(w)
    n = ML_H * CHUNK
    heads = [d * ML_H + h for h in range(ML_H)]
    rows_of = lambda pieces: jnp.concatenate(pieces, axis=0)
    cols_of = lambda pieces: jnp.concatenate(pieces, axis=1)
    f_c = rows_of([w[:, 8 + j:9 + j] for j in heads])
    i_c = rows_of([w[:, j:j + 1] for j in heads])
    f_r = cols_of([wt[8 + j:9 + j, :] for j in heads])
    i_r = cols_of([wt[j:j + 1, :] for j in heads])
    f_l = [ftot[:, 8 + j:9 + j] for j in heads]
    m_prev = [m_ref[j:j + 1, 0:1] for j in heads]
    col1 = lambda vals: rows_of([jnp.broadcast_to(x, (CHUNK, 1)) for x in vals])
    row = lax.broadcasted_iota(jnp.int32, (n, n), 0)
    col = lax.broadcasted_iota(jnp.int32, (n, n), 1)
    t_row, t_col = row % CHUNK, col % CHUNK
    incl = ((row // CHUNK) == (col // CHUNK)) & ((t_col >= t_row) if reverse else (t_col <= t_row))
    row_k = lax.broadcasted_iota(jnp.int32, (n, ML_H * ML_DK), 0)
    col_k = lax.broadcasted_iota(jnp.int32, (n, ML_H * ML_DK), 1)
    head_k = (row_k // CHUNK) == (col_k // ML_DK)
    head_v = (row // CHUNK) == (col // ML_DV)
    q_bd = jnp.where(head_k, jnp.concatenate([qk[:, 0:128]] * ML_H, axis=0), 0.0)
    k_bd = jnp.where(head_k, jnp.concatenate([qk[:, 128:256] * (ML_DK ** -0.5)] * ML_H, axis=0), 0.0)
    v_bd = jnp.where(head_v, jnp.concatenate([v] * ML_H, axis=0), 0.0)
    c_prev = c_ref[d]
    n_prev = cols_of([n_ref[j:j + 1, :] for j in heads])
    dm = jnp.where(incl, f_c - f_r + i_r, -jnp.inf)
    inter = f_c + col1(m_prev)
    mt = jnp.maximum(inter, jnp.max(dm, axis=-1, keepdims=True))
    wi = jnp.exp(dm - mt)
    we = jnp.exp(inter - mt)
    q_b = q_bd.astype(BF16)
    k_b = k_bd.astype(BF16)
    s = _dot_nt(q_b, k_b) * wi
    num = we * _dot_nt(q_b, c_prev.astype(BF16)) + _mm(s, v_bd)
    den = we * jnp.sum(q_bd * n_prev, axis=-1, keepdims=True) + jnp.sum(s, axis=-1, keepdims=True)
    h_bd = num / jnp.maximum(jnp.abs(den), jnp.exp(-mt))
    out = h_bd[0:CHUNK]
    for h in range(1, ML_H):
        out = out + h_bd[h * CHUNK:(h + 1) * CHUNK]
    gs_c = col1(f_l) - f_c + i_c
    m_new = [jnp.maximum(f_l[h] + m_prev[h],
                         jnp.max(gs_c[h * CHUNK:(h + 1) * CHUNK], axis=0, keepdims=True)) for h in range(ML_H)]
    ws = jnp.exp(gs_c - col1(m_new))
    wc = [jnp.exp(f_l[h] + m_prev[h] - m_new[h]) for h in range(ML_H)]
    wc_rows = rows_of([jnp.broadcast_to(x, (ML_DV, 1)) for x in wc])
    wc_cols = cols_of([jnp.broadcast_to(x, (1, ML_DK)) for x in wc])
    c_ref[d] = wc_rows * c_prev + _dot_tn((v_bd * ws).astype(BF16), k_b)
    n_new = wc_cols * n_prev + jnp.sum(k_bd * ws, axis=0, keepdims=True)
    for h, j in enumerate(heads):
        n_ref[j:j + 1, :] = n_new[:, h * ML_DK:(h + 1) * ML_DK]
        m_ref[j:j + 1, :] = jnp.broadcast_to(m_new[h], (1, LANES))
    return out


def _mlstm_kernel(*refs, has_state):
    if has_state:
        (qk_ref, v_ref, og_ref, sm_ref, cw_ref, bias_ref, g_ref, c0_ref, n0_ref, m0_ref,
         o_ref, cf_ref, nf_ref, mf_ref, c_ref, n_ref, m_ref, h_ref) = refs
        c_ref[...] = jnp.zeros(c_ref.shape, F32)
        for d in range(2):
            for h in range(ML_H):
                c_ref[d, h * ML_DV:(h + 1) * ML_DV, h * ML_DK:(h + 1) * ML_DK] = c0_ref[0, d, h]
        n_ref[...] = n0_ref[0]
        m_ref[...] = m0_ref[0]
    else:
        (qk_ref, v_ref, og_ref, sm_ref, cw_ref, bias_ref, g_ref,
         o_ref, cf_ref, nf_ref, mf_ref, c_ref, n_ref, m_ref, h_ref) = refs
        c_ref[...] = jnp.zeros(c_ref.shape, F32)
        n_ref[...] = jnp.zeros(n_ref.shape, F32)
        m_ref[...] = jnp.zeros(m_ref.shape, F32)
    n_chunks = qk_ref.shape[0] // CHUNK
    h_ref[...] = jnp.zeros(h_ref.shape, F32)
    cw = cw_ref[...]
    bias = bias_ref[...]

    def step(j, carry):
        for d in range(2):
            c = j if d == 0 else n_chunks - 1 - j
            rows = pl.ds(pl.multiple_of(c * CHUNK, CHUNK), CHUNK)
            qk = _conv_chunk(qk_ref, cw, c, n_chunks, 0, 256)
            h_ref[rows, :] += _mlstm_chunk(qk, v_ref[rows, :], sm_ref[rows, :], d, bias,
                                           c_ref, n_ref, m_ref)
        return carry

    lax.fori_loop(0, n_chunks, step, 0)
    for d in range(2):
        for h in range(ML_H):
            cf_ref[0, d, h] = c_ref[d, h * ML_DV:(h + 1) * ML_DV, h * ML_DK:(h + 1) * ML_DK]
    nf_ref[0] = n_ref[...]
    mf_ref[0] = m_ref[...]
    gmat = _group_matrix(ML_H * ML_DV, ML_DV, 1.0 / ML_DV)

    def finish(c, carry):
        rows = pl.ds(pl.multiple_of(c * CHUNK, CHUNK), CHUNK)
        o_ref[rows, :] = (_head_rms(h_ref[rows, :], gmat, g_ref[...])
                          * _sigmoid(og_ref[rows, :])).astype(o_ref.dtype)
        return carry

    lax.fori_loop(0, n_chunks, finish, 0)


def mlstm_branch(u, conv_w, p, layer, n_b, seq, row0, state0=None):
    i = layer
    rb = row0 // seq
    has_state = state0 is not None
    bias = jnp.zeros((1, LANES), F32)
    bias = bias.at[0, 0:8].set(p['ml_i_bias'][i].reshape(8)).at[0, 8:16].set(p['ml_f_bias'][i].reshape(8))
    full = lambda *shape: pl.BlockSpec(shape, lambda b: (0,) * len(shape))
    in_specs = [pl.BlockSpec((seq, 256), lambda b: (rb + b, U_MLQK // 256)),
                pl.BlockSpec((seq, 256), lambda b: (rb + b, U_MLVO // 256)),
                pl.BlockSpec((seq, 256), lambda b: (rb + b, U_MLVO // 256 + 1)),
                pl.BlockSpec((seq, LANES), lambda b: (rb + b, U_SMALL // LANES)),
                full(3, 256), full(1, LANES), full(1, 256)]
    args = [u, u, u, u, conv_w[:, U_MLQK:U_MLQK + 256], bias, p['ml_norm_g'][i].reshape(1, 256)]
    if has_state:
        c0, n0, m0 = state0
        in_specs += [pl.BlockSpec((1, 2, ML_H, ML_DV, ML_DK), lambda b: (b, 0, 0, 0, 0)),
                     pl.BlockSpec((1, 8, ML_DK), lambda b: (b, 0, 0)),
                     pl.BlockSpec((1, 8, LANES), lambda b: (b, 0, 0))]
        args += [c0, n0.reshape(n_b, 8, ML_DK),
                 jnp.broadcast_to(m0.reshape(n_b, 8, 1), (n_b, 8, LANES))]
    o, cf, nf, mf = pl.pallas_call(
        functools.partial(_mlstm_kernel, has_state=has_state),
        grid=(n_b,),
        in_specs=in_specs,
        out_specs=[pl.BlockSpec((seq, 256), lambda b: (b, 0)),
                   pl.BlockSpec((1, 2, ML_H, ML_DV, ML_DK), lambda b: (b, 0, 0, 0, 0)),
                   pl.BlockSpec((1, 8, ML_DK), lambda b: (b, 0, 0)),
                   pl.BlockSpec((1, 8, LANES), lambda b: (b, 0, 0))],
        out_shape=[jax.ShapeDtypeStruct((n_b * seq, 256), BF16),
                   jax.ShapeDtypeStruct((n_b, 2, ML_H, ML_DV, ML_DK), F32),
                   jax.ShapeDtypeStruct((n_b, 8, ML_DK), F32),
                   jax.ShapeDtypeStruct((n_b, 8, LANES), F32)],
        scratch_shapes=[pltpu.VMEM((2, ML_H * ML_DV, ML_H * ML_DK), F32), pltpu.VMEM((8, ML_DK), F32),
                        pltpu.VMEM((8, LANES), F32), pltpu.VMEM((seq, 256), F32)],
        compiler_params=_cparams(("parallel",)),
    )(*args)
    return o, cf, nf.reshape(n_b, 2, ML_H, ML_DK), mf[:, :, 0].reshape(n_b, 2, ML_H)


def _gla_chunk(qk, v, small, d, a2_ref, abias, st_ref):
    reverse = d == 1
    q = qk[:, 0:128] * (GL_DK ** -0.5)
    k = qk[:, 128:256]
    la = _log_sigmoid(_dot(small, a2_ref[d], HI) + abias[d:d + 1]) * (1.0 / GL_TAU)
    cum, total = _time_cumsum(la, reverse)
    mid = cum[CHUNK // 2:CHUNK // 2 + 1]
    qe = q * jnp.exp(cum - mid)
    ke = k * jnp.exp(mid - cum)
    qg = q * jnp.exp(cum)
    kg = k * jnp.exp(total - cum)
    g_end = jnp.exp(total)
    n = GL_H * CHUNK
    row = lax.broadcasted_iota(jnp.int32, (n, n), 0)
    col = lax.broadcasted_iota(jnp.int32, (n, n), 1)
    t_row, t_col = row % CHUNK, col % CHUNK
    incl = ((row // CHUNK) == (col // CHUNK)) & ((t_col >= t_row) if reverse else (t_col <= t_row))
    row_k = lax.broadcasted_iota(jnp.int32, (n, GL_H * GL_DK), 0)
    col_k = lax.broadcasted_iota(jnp.int32, (n, GL_H * GL_DK), 1)
    head_k = (row_k // CHUNK) == (col_k // GL_DK)
    head_v = (row // CHUNK) == (col // GL_DV)

    def bd(x, mask):
        xb = x.astype(BF16)
        return jnp.where(mask, jnp.concatenate([xb] * GL_H, axis=0), jnp.zeros(mask.shape, BF16))

    v_bd = bd(v, head_v)
    a = jnp.where(incl, _dot_nt(bd(qe, head_k), bd(ke, head_k)), 0.0)
    st = st_ref[d]
    o_bd = _dot_nt(bd(qg, head_k), st.astype(BF16)) + _dot(a.astype(BF16), v_bd)
    st_ref[d] = st * g_end + _dot_tn(v_bd, bd(kg, head_k))
    out = o_bd[0:CHUNK]
    for h in range(1, GL_H):
        out = out + o_bd[h * CHUNK:(h + 1) * CHUNK]
    return out


def _gla_kernel(*refs, has_state):
    if has_state:
        (qk_ref, v_ref, gg_ref, sm_ref, cw_ref, a2_ref, ab_ref, g_ref, s0_ref,
         o_ref, sf_ref, st_ref, acc_ref) = refs
        st_ref[...] = jnp.zeros(st_ref.shape, F32)
        for d in range(2):
            for h in range(GL_H):
                st_ref[d, h * GL_DV:(h + 1) * GL_DV, h * GL_DK:(h + 1) * GL_DK] = s0_ref[0, d, h]
    else:
        (qk_ref, v_ref, gg_ref, sm_ref, cw_ref, a2_ref, ab_ref, g_ref,
         o_ref, sf_ref, st_ref, acc_ref) = refs
        st_ref[...] = jnp.zeros(st_ref.shape, F32)
    n_chunks = qk_ref.shape[0] // CHUNK
    acc_ref[...] = jnp.zeros(acc_ref.shape, F32)
    cw = cw_ref[...]
    abias = ab_ref[...]

    def step(j, carry):
        for d in range(2):
            c = j if d == 0 else n_chunks - 1 - j
            rows = pl.ds(pl.multiple_of(c * CHUNK, CHUNK), CHUNK)
            qk = _conv_chunk(qk_ref, cw, c, n_chunks, 0, 256)
            acc_ref[rows, :] += _gla_chunk(qk, v_ref[rows, :], sm_ref[rows, :], d, a2_ref, abias, st_ref)
        return carry

    lax.fori_loop(0, n_chunks, step, 0)
    for d in range(2):
        for h in range(GL_H):
            sf_ref[0, d, h] = st_ref[d, h * GL_DV:(h + 1) * GL_DV, h * GL_DK:(h + 1) * GL_DK]
    gmat = _group_matrix(GL_H * GL_DV, GL_DV, 1.0 / GL_DV)

    def finish(c, carry):
        rows = pl.ds(pl.multiple_of(c * CHUNK, CHUNK), CHUNK)
        gg = gg_ref[rows, :]
        o_ref[rows, :] = (_head_rms(acc_ref[rows, :], gmat, g_ref[...]) * (gg * _sigmoid(gg))).astype(o_ref.dtype)
        return carry

    lax.fori_loop(0, n_chunks, finish, 0)


def gla_branch(u, conv_w, p, layer, n_b, seq, row0, state0=None):
    i = layer
    rb = row0 // seq
    has_state = state0 is not None
    a2 = jnp.zeros((2, LANES, GL_H * GL_DK), F32).at[:, 16:32, :].set(p['gl_a2'][i])
    full = lambda *shape: pl.BlockSpec(shape, lambda b: (0,) * len(shape))
    in_specs = [pl.BlockSpec((seq, 256), lambda b: (rb + b, U_GLQK // 256)),
                pl.BlockSpec((seq, 256), lambda b: (rb + b, U_GLVG // 256)),
                pl.BlockSpec((seq, 256), lambda b: (rb + b, U_GLVG // 256 + 1)),
                pl.BlockSpec((seq, LANES), lambda b: (rb + b, U_SMALL // LANES)),
                full(3, 256), full(2, LANES, GL_H * GL_DK), full(2, GL_H * GL_DK), full(1, 256)]
    args = [u, u, u, u, conv_w[:, U_GLQK:U_GLQK + 256], a2, p['gl_a_bias'][i],
            p['gl_norm_g'][i].reshape(1, 256)]
    if has_state:
        in_specs.append(pl.BlockSpec((1, 2, GL_H, GL_DV, GL_DK), lambda b: (b, 0, 0, 0, 0)))
        args.append(jnp.swapaxes(state0, -1, -2))
    o, sf = pl.pallas_call(
        functools.partial(_gla_kernel, has_state=has_state),
        grid=(n_b,),
        in_specs=in_specs,
        out_specs=[pl.BlockSpec((seq, 256), lambda b: (b, 0)),
                   pl.BlockSpec((1, 2, GL_H, GL_DV, GL_DK), lambda b: (b, 0, 0, 0, 0))],
        out_shape=[jax.ShapeDtypeStruct((n_b * seq, 256), BF16),
                   jax.ShapeDtypeStruct((n_b, 2, GL_H, GL_DV, GL_DK), F32)],
        scratch_shapes=[pltpu.VMEM((2, GL_H * GL_DV, GL_H * GL_DK), F32), pltpu.VMEM((seq, 256), F32)],
        compiler_params=_cparams(("parallel",)),
    )(*args)
    return o, jnp.swapaxes(sf, -1, -2)


RW_SEQS_PER_STEP = 2


def rwkv_branch(u, conv_w, p, layer, n_b, seq, row0, state0=None):
    i = layer
    g = RW_SEQS_PER_STEP
    rows = g * seq
    rb = row0 // rows
    vec = jnp.stack([p['rw_k_k'][i], p['rw_k_a'][i], p['rw_r_k'][i], p['rw_norm_g'][i]])
    has_state = state0 is not None
    full = lambda *shape: pl.BlockSpec(shape, lambda b: (0,) * len(shape))
    in_specs = [pl.BlockSpec((rows, 1024), lambda b: (rb + b, 0), pipeline_mode=pl.Buffered(1)),
                full(3, 1024), full(2, RW_W), full(2, 64, RW_W), full(2, RW_W), full(2, 64, RW_W),
                full(128, RW_W), full(4, RW_W)]
    args = [u, conv_w[:, 0:1024], p['rw_w0'][i], p['rw_w2'][i], p['rw_a0'][i], p['rw_a2'][i],
            p['rw_g2'][i], vec]
    if has_state:
        in_specs.append(pl.BlockSpec((g, 2, RW_H, RW_N, RW_N), lambda b: (b, 0, 0, 0, 0)))
        args.append(state0)
    return pl.pallas_call(
        functools.partial(_rwkv_kernel, has_state=has_state),
        grid=(n_b // g,),
        in_specs=in_specs,
        out_specs=[pl.BlockSpec((rows, RW_W), lambda b: (b, 0)),
                   pl.BlockSpec((g, 2, RW_H, RW_N, RW_N), lambda b: (b, 0, 0, 0, 0))],
        out_shape=[jax.ShapeDtypeStruct((n_b * seq, RW_W), BF16),
                   jax.ShapeDtypeStruct((n_b, 2, RW_H, RW_N, RW_N), F32)],
        scratch_shapes=([pltpu.VMEM((RW_W, RW_W), F32)] * (2 * g) + [pltpu.VMEM((rows, RW_W), F32)] * 4),
        compiler_params=_cparams(("parallel",)),
    )(*args)


TOKEN_TILE = 256


def kernel(x_prompt, x_sample, cache_attn_k, cache_attn_v, state_rwkv, state_mlstm_c, state_mlstm_n,
           state_mlstm_m, state_gla, c, c_ctx, norm1_g, norm2_g, final_g, w_ada, b_ada, w_in, conv_w,
           da_lam_q1, da_lam_k1, da_lam_q2, da_lam_k2, da_norm_g, rw_w0, rw_w2, rw_a0, rw_a2, rw_g2,
           rw_k_k, rw_k_a, rw_r_k, rw_norm_g, ml_i_bias, ml_f_bias, ml_norm_g, gl_a2, gl_a_bias, gl_norm_g,
           w_branch, w_bgate, w_out, w_router, b_router, w_gu, b_gu, w_dn, b_dn):
    p = dict(rw_w0=rw_w0, rw_w2=rw_w2, rw_a0=rw_a0, rw_a2=rw_a2, rw_g2=rw_g2, rw_k_k=rw_k_k, rw_k_a=rw_k_a,
             rw_r_k=rw_r_k, rw_norm_g=rw_norm_g, ml_i_bias=ml_i_bias, ml_f_bias=ml_f_bias,
             ml_norm_g=ml_norm_g, gl_a2=gl_a2, gl_a_bias=gl_a_bias, gl_norm_g=gl_norm_g)
    n_cb, n_cl, d = x_prompt.shape
    n_db, n_dl, _ = x_sample.shape
    depth = w_in.shape[0]
    n_ctx = n_cb * n_cl
    n_tok = n_ctx + n_db * n_dl
    tm = TOKEN_TILE
    ctx_row = n_db

    def mod_index(i):
        return jnp.where(i < n_ctx // tm, ctx_row, (i - n_ctx // tm) * tm // n_dl)

    n_mod = 16
    cvecs = jnp.zeros((n_mod, d), F32).at[:n_db].set(c).at[ctx_row].set(c_ctx)
    mods = adaln_all(cvecs, w_ada, b_ada)
    mods = jnp.pad(mods.reshape(depth, n_mod, 6, d), ((0, 0), (0, 0), (0, 2), (0, 0)))

    x = jnp.concatenate([x_prompt.reshape(n_ctx, d), x_sample.reshape(n_db * n_dl, d)], axis=0)
    states = []
    for l in range(depth):
        mod = mods[l]
        u, h = in_projection(x, mod, mod_index, norm1_g[l], _permute_w_in(w_in[l]).astype(BF16), tm)
        lam_p = jnp.stack([da_lam_q1[l], da_lam_k1[l], da_lam_q2[l], da_lam_k2[l]])
        n_past = cache_attn_k.shape[2]
        o_da = (attention_ctx(u, lam_p, da_norm_g[l], l, n_cb, n_cl, 0),
                attention_dec(u, cache_attn_k[:, l].reshape(n_db, n_past, 256),
                              cache_attn_v[:, l].reshape(n_db, n_past, 256), lam_p, da_norm_g[l], l,
                              n_db, n_dl, n_ctx))
        o_rw_c, rw_s = rwkv_branch(u, conv_w[l], p, l, n_cb, n_cl, 0)
        o_rw_d, _ = rwkv_branch(u, conv_w[l], p, l, n_db, n_dl, n_ctx, state_rwkv[:, l])
        o_ml_c, ml_c, ml_n, ml_m = mlstm_branch(u, conv_w[l], p, l, n_cb, n_cl, 0)
        o_ml_d = mlstm_branch(u, conv_w[l], p, l, n_db, n_dl, n_ctx,
                              (state_mlstm_c[:, l], state_mlstm_n[:, l], state_mlstm_m[:, l]))[0]
        o_gl_c, gl_s = gla_branch(u, conv_w[l], p, l, n_cb, n_cl, 0)
        o_gl_d, _ = gla_branch(u, conv_w[l], p, l, n_db, n_dl, n_ctx, state_gla[:, l])
        outs_ctx = [o_da[0], o_rw_c, o_ml_c, o_gl_c]
        outs_dec = [o_da[1], o_rw_d, o_ml_d, o_gl_d]
        new_k = u[:n_ctx, U_DA + 256:U_DA + 512].reshape(n_cb, n_cl, DA_H, 2, DA_DK)
        new_v = u[:n_ctx, U_DA + 512:U_DA + 768].reshape(n_cb, n_cl, DA_H, DA_DV)
        states.append((new_k, new_v, rw_s, ml_c, ml_n, ml_m, gl_s))

        x1, h3, route, tile_counts = merge_and_route(x, h, outs_ctx, outs_dec, mod, mod_index,
                                                     w_branch[l].astype(BF16), w_bgate[l].astype(BF16),
                                                     w_out[l].astype(BF16), norm2_g[l], w_router[l],
                                                     b_router[l], tm)
        block_e, row_slot, n_act, row_gate = _route_metadata(route, tile_counts, n_tok)
        ys3 = moe_experts(h3, block_e, row_slot, n_act, row_gate, w_gu[l], b_gu[l], w_dn[l], b_dn[l],
                          n_tok * TOP_K)
        x = moe_combine(ys3, x1, mod, mod_index, final_g, tm, final=(l == depth - 1))

    y_prompt = x[:n_ctx].reshape(n_cb, n_cl, d)
    y_sample = x[n_ctx:].reshape(n_db, n_dl, d)
    stacked = tuple(jnp.stack([s[j] for s in states], axis=1) for j in range(7))
    return (y_prompt, y_sample) + stacked
```

```python
import functools
import math

import numpy as np
import jax
import jax.numpy as jnp
from jax import lax
from jax.experimental import pallas as pl
from jax.experimental.pallas import tpu as pltpu

F32 = jnp.float32
BF16 = jnp.bfloat16

D_MODEL = 1024
GRID_W = 64
N_BRANCH = 4
BRANCH_W = 256
DA_H, DA_DK, DA_DV = 4, 32, 64
RW_H, RW_N = 4, 64
RW_W = RW_H * RW_N
ML_H, ML_DK, ML_DV = 4, 32, 64
GL_H, GL_DK, GL_DV = 4, 32, 64
GL_TAU = 16.0
CHUNK = 64
N_EXPERTS = 32
TOP_K = 4
D_FF = 1024
SWIGLU_ALPHA = 1.702
SWIGLU_LIMIT = 7.0
ROPE_BASE = 10000.0
EPS = 1e-6

LANES = 128
VMEM_LIMIT = 56 * 1024 * 1024

U_RW = 0
U_MLQK = 1024
U_GLQK = 1280
N_CONV = 1536
U_DA = 1536
U_MLVO = 2304
U_GLVG = 2816
U_SMALL = 3328
N_U = 3456

_REF_SPLITS = (('rw_r', 256), ('rw_k', 256), ('rw_v', 256), ('rw_wd', 64), ('rw_ad', 64), ('rw_gd', 128),
               ('ml_q', 128), ('ml_k', 128), ('gl_q', 128), ('gl_k', 128),
               ('da_q', 256), ('da_k', 256), ('da_v', 256), ('ml_v', 256), ('ml_o', 256), ('ml_i', 8),
               ('ml_f', 8), ('gl_v', 256), ('gl_a', 16), ('gl_g', 256))
_REF_OFF = {}
_o = 0
for _n, _s in _REF_SPLITS:
    _REF_OFF[_n] = (_o, _s)
    _o += _s
_MY_ORDER = ('rw_r', 'rw_k', 'rw_v', 'rw_wd', 'rw_ad', 'rw_gd', 'ml_q', 'ml_k', 'gl_q', 'gl_k',
             'da_q', 'da_k', 'da_v', 'ml_v', 'ml_o', 'gl_v', 'gl_g', 'ml_i', 'ml_f', 'gl_a')


def _permute_w_in(w):
    cols = [w[:, _REF_OFF[n][0]:_REF_OFF[n][0] + _REF_OFF[n][1]] for n in _MY_ORDER]
    cols.append(jnp.zeros((w.shape[0], N_U - sum(_REF_OFF[n][1] for n in _MY_ORDER)), w.dtype))
    return jnp.concatenate(cols, axis=1)


def _cparams(sem):
    return pltpu.CompilerParams(dimension_semantics=sem, vmem_limit_bytes=VMEM_LIMIT)


def _sigmoid(x):
    return 1.0 / (1.0 + jnp.exp(-x))


def _log_sigmoid(x):
    return jnp.minimum(x, 0.0) - jnp.log(1.0 + jnp.exp(-jnp.abs(x)))


def _dot(a, b, prec=None):
    return jnp.dot(a, b, preferred_element_type=F32, precision=prec)


def _dot_nt(a, b, prec=None):
    return lax.dot_general(a, b, (((1,), (1,)), ((), ())), preferred_element_type=F32, precision=prec)


def _dot_tn(a, b, prec=None):
    return lax.dot_general(a, b, (((0,), (0,)), ((), ())), preferred_element_type=F32, precision=prec)


HI = lax.Precision.HIGHEST


def _adaln_kernel(c_ref, w_ref, b_ref, o_ref):
    c = c_ref[...]
    s = c * _sigmoid(c)
    o_ref[0] = _dot(s, w_ref[0], HI) + b_ref[0]


def adaln_all(cvecs, w_ada, b_ada):
    depth, d, n = w_ada.shape
    r = cvecs.shape[0]
    tn = 1536
    return pl.pallas_call(
        _adaln_kernel,
        grid=(depth, n // tn),
        in_specs=[pl.BlockSpec((r, d), lambda l, j: (0, 0)),
                  pl.BlockSpec((1, d, tn), lambda l, j: (l, 0, j)),
                  pl.BlockSpec((1, 1, tn), lambda l, j: (l, 0, j))],
        out_specs=pl.BlockSpec((1, r, tn), lambda l, j: (l, 0, j)),
        out_shape=jax.ShapeDtypeStruct((depth, r, n), F32),
        compiler_params=_cparams(("arbitrary", "arbitrary")),
    )(cvecs, w_ada, b_ada.reshape(depth, 1, n))


def _rms(x, g):
    return x * lax.rsqrt(jnp.mean(x * x, axis=-1, keepdims=True) + EPS) * g


def _inproj_kernel(x_ref, mod_ref, g_ref, w_ref, u_ref, h_ref):
    x = x_ref[...]
    mod = mod_ref[0]
    h = _rms(x, g_ref[...]) * (1.0 + mod[1:2]) + mod[0:1]
    hb = h.astype(BF16)
    h_ref[...] = hb
    u_ref[...] = _dot(hb, w_ref[...])


def in_projection(x, mod, mod_index, norm_g, w_in_b, tm):
    t, d = x.shape
    return pl.pallas_call(
        _inproj_kernel,
        grid=(t // tm,),
        in_specs=[pl.BlockSpec((tm, d), lambda i: (i, 0)),
                  pl.BlockSpec((1, 8, d), lambda i: (mod_index(i), 0, 0)),
                  pl.BlockSpec((1, d), lambda i: (0, 0)),
                  pl.BlockSpec((d, N_U), lambda i: (0, 0))],
        out_specs=[pl.BlockSpec((tm, N_U), lambda i: (i, 0)),
                   pl.BlockSpec((tm, d), lambda i: (i, 0))],
        out_shape=[jax.ShapeDtypeStruct((t, N_U), F32), jax.ShapeDtypeStruct((t, d), BF16)],
        compiler_params=_cparams(("parallel",)),
    )(x, mod, norm_g.reshape(1, d), w_in_b)


ROUTE_W = LANES


def _to_tiles(o_ref, y):
    for s in range(y.shape[1] // LANES):
        o_ref[:, s, :] = y[:, s * LANES:(s + 1) * LANES]


def _from_tiles(x3):
    return jnp.concatenate([x3[:, s, :] for s in range(x3.shape[1])], axis=1)


def _merge_kernel(x_ref, h_ref, oda_c, oda_d, orw_c, orw_d, oml_c, oml_d, ogl_c, ogl_d, mod_ref, wbr_ref, wbg_ref,
                  wout_ref, g2_ref, wr_ref, br_ref, x1_ref, h2_ref, route_ref, cnt_ref, *, n_ctx_tiles):
    d = x_ref.shape[1]
    h = h_ref[...]
    mod = mod_ref[0]
    is_ctx = pl.program_id(0) < n_ctx_tiles
    merged = jnp.zeros(x_ref.shape, F32)
    for gi, (oc_ref, od_ref) in enumerate(((oda_c, oda_d), (orw_c, orw_d), (oml_c, oml_d), (ogl_c, ogl_d))):
        gate = _sigmoid(_dot(h, wbg_ref[:, gi * d:(gi + 1) * d]))
        o = jnp.where(is_ctx, oc_ref[...], od_ref[...])
        merged = merged + gate * _dot(o, wbr_ref[gi])
    y = _dot(merged.astype(BF16), wout_ref[...])
    x1 = x_ref[...] + mod[2:3] * y
    x1_ref[...] = x1
    h2 = _rms(x1, g2_ref[...]) * (1.0 + mod[4:5]) + mod[3:4]
    _to_tiles(h2_ref, h2)
    lane = lax.broadcasted_iota(jnp.int32, (1, ROUTE_W), 1)
    logits = jnp.where(lane < N_EXPERTS, _dot(h2.astype(BF16), wr_ref[...]) + br_ref[...], -jnp.inf)
    vals, ids = [], []
    for _ in range(TOP_K):
        m = jnp.max(logits, axis=-1, keepdims=True)
        idx = jnp.min(jnp.where(logits == m, lane, ROUTE_W), axis=-1, keepdims=True)
        vals.append(m)
        ids.append(idx)
        logits = jnp.where(lane == idx, -jnp.inf, logits)
    es = [jnp.exp(v - vals[0]) for v in vals]
    inv = 1.0 / sum(es)
    route = jnp.zeros((x_ref.shape[0], ROUTE_W), F32)
    for r in range(TOP_K):
        route = jnp.where(lane == r, es[r] * inv, route)
        route = jnp.where(lane == TOP_K + r, ids[r].astype(F32), route)
    route_ref[...] = route
    hist = sum(jnp.sum((lane == ids[r]).astype(F32), axis=0, keepdims=True) for r in range(TOP_K))
    cnt_ref[0] = jnp.broadcast_to(hist, cnt_ref.shape[1:])


def merge_and_route(x, h, outs_ctx, outs_dec, mod, mod_index, w_branch_b, w_bgate_b, w_out_b, norm2_g, w_router,
                    b_router, tm):
    t, d = x.shape
    nct = outs_ctx[0].shape[0] // tm
    ctx_row = pl.BlockSpec((tm, BRANCH_W), lambda i: (jnp.minimum(i, nct - 1), 0))
    dec_row = pl.BlockSpec((tm, BRANCH_W), lambda i: (jnp.maximum(i - nct, 0), 0))
    wr = jnp.zeros((d, ROUTE_W), BF16).at[:, :N_EXPERTS].set(w_router.astype(BF16))
    br = jnp.zeros((1, ROUTE_W), F32).at[0, :N_EXPERTS].set(b_router)
    row = lambda w: pl.BlockSpec((tm, w), lambda i: (i, 0))
    full = lambda *shape: pl.BlockSpec(shape, lambda i: (0,) * len(shape))
    return pl.pallas_call(
        functools.partial(_merge_kernel, n_ctx_tiles=nct),
        grid=(t // tm,),
        in_specs=[row(d), row(d)] + [ctx_row, dec_row] * N_BRANCH + [
                  pl.BlockSpec((1, 8, d), lambda i: (mod_index(i), 0, 0)),
                  full(N_BRANCH, BRANCH_W, d), full(d, N_BRANCH * d), full(d, d), full(1, d),
                  full(d, ROUTE_W), full(1, ROUTE_W)],
        out_specs=[row(d), pl.BlockSpec((tm, d // LANES, LANES), lambda i: (i, 0, 0)), row(ROUTE_W),
                   pl.BlockSpec((1, 8, ROUTE_W), lambda i: (i, 0, 0))],
        out_shape=[jax.ShapeDtypeStruct((t, d), F32), jax.ShapeDtypeStruct((t, d // LANES, LANES), F32),
                   jax.ShapeDtypeStruct((t, ROUTE_W), F32), jax.ShapeDtypeStruct((t // tm, 8, ROUTE_W), F32)],
        compiler_params=_cparams(("parallel",)),
    )(x, h, *[o for pair in zip(outs_ctx, outs_dec) for o in pair], mod, w_branch_b, w_bgate_b, w_out_b,
      norm2_g.reshape(1, d), wr, br)


MOE_BLOCK = 256
ISSUE_UNROLL = 8
assert TOP_K == 4


def _moe_kernel(be_ref, slot_ref, nact_ref, h_hbm, gate_ref, wg_ref, bg_ref, wd_ref, bd_ref, y_hbm,
                xbuf, ybuf, wg_b, wd_b, sem_in, sem_out):
    i = pl.program_id(0)
    n_act = nact_ref[0]
    blk = MOE_BLOCK
    n_tok = h_hbm.shape[0]
    cur = i % 2

    def start_gather(block, buf_slot):
        def issue(r, carry):
            tok = jnp.minimum(lax.shift_right_logical(slot_ref[block * blk + r], 2), n_tok - 1)
            pltpu.make_async_copy(h_hbm.at[tok], xbuf.at[buf_slot, r], sem_in.at[buf_slot]).start()
            return carry
        lax.fori_loop(0, blk, issue, 0, unroll=ISSUE_UNROLL)

    def wait_gather(buf_slot):
        pltpu.make_async_copy(h_hbm.at[pl.ds(0, blk)], xbuf.at[buf_slot], sem_in.at[buf_slot]).wait()

    def start_scatter(block, buf_slot):
        def issue(r, carry):
            pltpu.make_async_copy(ybuf.at[buf_slot, r], y_hbm.at[slot_ref[block * blk + r]],
                                  sem_out.at[buf_slot]).start()
            return carry
        lax.fori_loop(0, blk, issue, 0, unroll=ISSUE_UNROLL)

    def wait_scatter(buf_slot):
        pltpu.make_async_copy(ybuf.at[buf_slot], y_hbm.at[pl.ds(0, blk)], sem_out.at[buf_slot]).wait()

    @pl.when(i == 0)
    def _():
        n_slots = y_hbm.shape[0] - 2 * blk
        ybuf[0] = jnp.zeros(ybuf.shape[1:], F32)
        for bank in range(2):
            fill = pltpu.make_async_copy(ybuf.at[0], y_hbm.at[pl.ds(n_slots + bank * blk, blk)], sem_out.at[0])
            fill.start()
            fill.wait()

    @pl.when((i == 0) & (n_act > 0))
    def _():
        start_gather(0, 0)

    @pl.when(i < n_act)
    def _():
        @pl.when(i + 1 < n_act)
        def _():
            start_gather(i + 1, 1 - cur)

        @pl.when((i == 0) | (be_ref[i] != be_ref[jnp.maximum(i - 1, 0)]))
        def _():
            rows = 128

            def cast(c, carry):
                sl = pl.ds(pl.multiple_of(c * rows, rows), rows)
                wg_b[sl, :] = wg_ref[0, sl, :].astype(BF16)
                wd_b[sl, :] = wd_ref[0, sl, :].astype(BF16)
                return carry

            lax.fori_loop(0, wg_b.shape[0] // rows, cast, 0)

        wait_gather(cur)
        x = _from_tiles(xbuf.at[cur]).astype(BF16)
        gu = _dot(x, wg_b[...]) + bg_ref[0]
        glu = jnp.minimum(gu[:, :D_FF], SWIGLU_LIMIT)
        lin = jnp.clip(gu[:, D_FF:], -SWIGLU_LIMIT, SWIGLU_LIMIT)
        act = (lin + 1.0) * glu * _sigmoid(SWIGLU_ALPHA * glu)
        y = (_dot(act.astype(BF16), wd_b[...]) + bd_ref[0]) * gate_ref[...]

        @pl.when(i >= 2)
        def _():
            wait_scatter(cur)

        _to_tiles(ybuf.at[cur], y)
        start_scatter(i, cur)

    @pl.when(i == pl.num_programs(0) - 1)
    def _():
        @pl.when(n_act >= 1)
        def _():
            wait_scatter((n_act - 1) % 2)

        @pl.when(n_act >= 2)
        def _():
            wait_scatter(n_act % 2)


def moe_experts(h3, block_e, row_slot, n_act, row_gate, w_gu, b_gu, w_dn, b_dn, n_slots):
    n_rows = row_slot.shape[0]
    n_blocks = n_rows // MOE_BLOCK
    e, d, f2 = w_gu.shape
    s = d // LANES
    grid_spec = pltpu.PrefetchScalarGridSpec(
        num_scalar_prefetch=3,
        grid=(n_blocks,),
        in_specs=[pl.BlockSpec(memory_space=pl.ANY),
                  pl.BlockSpec((MOE_BLOCK, 1), lambda i, be, sl, na: (i, 0)),
                  pl.BlockSpec((1, d, f2), lambda i, be, sl, na: (be[i], 0, 0)),
                  pl.BlockSpec((1, 1, f2), lambda i, be, sl, na: (be[i], 0, 0)),
                  pl.BlockSpec((1, f2 // 2, d), lambda i, be, sl, na: (be[i], 0, 0)),
                  pl.BlockSpec((1, 1, d), lambda i, be, sl, na: (be[i], 0, 0))],
        out_specs=pl.BlockSpec(memory_space=pl.ANY),
        scratch_shapes=[pltpu.VMEM((2, MOE_BLOCK, s, LANES), F32), pltpu.VMEM((2, MOE_BLOCK, s, LANES), F32),
                        pltpu.VMEM((d, f2), BF16), pltpu.VMEM((f2 // 2, d), BF16),
                        pltpu.SemaphoreType.DMA((2,)), pltpu.SemaphoreType.DMA((2,))],
    )
    return pl.pallas_call(
        _moe_kernel,
        grid_spec=grid_spec,
        out_shape=jax.ShapeDtypeStruct((n_slots + 2 * MOE_BLOCK, s, LANES), F32),
        compiler_params=_cparams(("arbitrary",)),
    )(block_e, row_slot, n_act, h3, row_gate.reshape(n_rows, 1), w_gu, b_gu.reshape(e, 1, f2),
      w_dn, b_dn.reshape(e, 1, d))


def _combine_kernel(y_ref, x1_ref, mod_ref, fg_ref, o_ref, *, final):
    tc = x1_ref.shape[0]
    y = jnp.concatenate(
        [sum(y_ref[pl.ds(j, tc, stride=TOP_K), s, :] for j in range(TOP_K)) for s in range(y_ref.shape[1])],
        axis=1)
    x2 = x1_ref[...] + mod_ref[0][5:6] * y
    if final:
        x2 = _rms(x2, fg_ref[...])
    o_ref[...] = x2


def moe_combine(ys3, x1, mod, mod_index, final_g, tc, final):
    t, d = x1.shape
    s = d // LANES
    return pl.pallas_call(
        functools.partial(_combine_kernel, final=final),
        grid=(t // tc,),
        in_specs=[pl.BlockSpec((tc * TOP_K, s, LANES), lambda i: (i, 0, 0)),
                  pl.BlockSpec((tc, d), lambda i: (i, 0)),
                  pl.BlockSpec((1, 8, d), lambda i: (mod_index(i), 0, 0)),
                  pl.BlockSpec((1, d), lambda i: (0, 0))],
        out_specs=pl.BlockSpec((tc, d), lambda i: (i, 0)),
        out_shape=jax.ShapeDtypeStruct((t, d), F32),
        compiler_params=_cparams(("parallel",)),
    )(ys3, x1, mod, final_g.reshape(1, d))


def _route_metadata(route, tile_counts, n_tok):
    gates = route[:, 0:TOP_K].reshape(-1)
    experts = route[:, TOP_K:2 * TOP_K].astype(jnp.int32).reshape(-1)
    n_slots = n_tok * TOP_K
    n_rows = n_slots + N_EXPERTS * MOE_BLOCK
    n_blocks = n_rows // MOE_BLOCK
    order = jnp.argsort(experts, stable=True).astype(jnp.int32)
    counts = jnp.sum(tile_counts[:, 0, :N_EXPERTS], axis=0).astype(jnp.int32)
    sort_start = jnp.cumsum(counts) - counts
    padded = ((counts + MOE_BLOCK - 1) // MOE_BLOCK) * MOE_BLOCK
    pad_end = jnp.cumsum(padded)
    pad_start = pad_end - padded
    block_start = jnp.arange(n_blocks, dtype=jnp.int32) * MOE_BLOCK
    block_e = jnp.minimum(jnp.sum((pad_end[None, :] <= block_start[:, None]).astype(jnp.int32), axis=1),
                          N_EXPERTS - 1)
    row = jnp.arange(n_rows, dtype=jnp.int32)
    per_row = lambda table: jnp.repeat(table[block_e], MOE_BLOCK)
    rank = row - per_row(pad_start)
    valid = rank < per_row(counts)
    src = order[jnp.clip(per_row(sort_start) + rank, 0, n_slots - 1)]
    scratch_row = n_slots + ((row // MOE_BLOCK) % 2) * MOE_BLOCK + row % MOE_BLOCK
    row_slot = jnp.where(valid, src, scratch_row).astype(jnp.int32)
    row_gate = jnp.where(valid, gates[src], 0.0)
    n_act = (pad_end[-1] // MOE_BLOCK).astype(jnp.int32).reshape(1)
    return block_e, row_slot, n_act, row_gate


def _split3(x):
    hi = x.astype(BF16)
    r1 = x - hi.astype(F32)
    mid = r1.astype(BF16)
    lo = (r1 - mid.astype(F32)).astype(BF16)
    return hi, mid, lo


def _dot_exact_rhs(x, c_b):
    hi, mid, lo = _split3(x)
    return _dot(hi, c_b) + _dot(mid, c_b) + _dot(lo, c_b)


def _dot_exact_lhs(c_b, x):
    hi, mid, lo = _split3(x)
    return _dot(c_b, hi) + _dot(c_b, mid) + _dot(c_b, lo)


def _group_matrix(n, group, value):
    r = lax.broadcasted_iota(jnp.int32, (n, n), 0) // group
    c = lax.broadcasted_iota(jnp.int32, (n, n), 1) // group
    return jnp.where(r == c, value, 0.0).astype(BF16)


def _head_rms(x, gmat, g):
    ms = _dot_exact_rhs(x * x, gmat)
    return x * lax.rsqrt(ms + EPS) * g


def _conv3(u, w, first, last):
    n = u.shape[0]
    prev = jnp.where(first, 0.0, pltpu.roll(u, 1, 0))
    nxt = jnp.where(last, 0.0, pltpu.roll(u, n - 1, 0))
    return prev * w[0:1] + u * w[1:2] + nxt * w[2:3]


def _rope_tables(n_tok):
    half = DA_DK // 4
    freqs = (ROPE_BASE ** (-np.arange(half, dtype=np.float32) / half)).astype(np.float32)
    t = np.arange(n_tok)
    row = (t // GRID_W).astype(np.float32)
    col = (t % GRID_W).astype(np.float32)
    cos32 = np.zeros((n_tok, DA_DK), np.float64)
    sin32 = np.zeros((n_tok, DA_DK), np.float64)
    for base, pos in ((0, row), (2 * half, col)):
        ang = (pos[:, None] * freqs[None, :]).astype(np.float32).astype(np.float64)
        cos32[:, base:base + half] = np.cos(ang)
        cos32[:, base + half:base + 2 * half] = np.cos(ang)
        sin32[:, base:base + half] = -np.sin(ang)
        sin32[:, base + half:base + 2 * half] = np.sin(ang)
    reps = DA_H * 2
    return (jnp.asarray(np.tile(cos32, (1, reps)), F32), jnp.asarray(np.tile(sin32, (1, reps)), F32))


def _rope(x, cos, sin):
    n = x.shape[1]
    lane = lax.broadcasted_iota(jnp.int32, x.shape, 1)
    partner = jnp.where((lane % 16) < 8, pltpu.roll(x, n - 8, 1), pltpu.roll(x, 8, 1))
    return x * cos + partner * sin


def _attn_core(q, kall, vall, lam, g, o_ref, lam_init):
    tq = q.shape[0]
    lane = lax.broadcasted_iota(jnp.int32, (1, 256), 1)
    acc = jnp.zeros((tq, 256), F32)
    for h in range(DA_H):
        probs = []
        for m in range(2):
            c0 = h * 2 * DA_DK + m * DA_DK
            qm = jnp.where((lane >= c0) & (lane < c0 + DA_DK), q, 0.0).astype(BF16)
            s = _dot_nt(qm, kall)
            s = s - jnp.max(s, axis=-1, keepdims=True)
            p = jnp.exp(s)
            probs.append(p / jnp.sum(p, axis=-1, keepdims=True))
        a = (probs[0] - lam * probs[1]).astype(BF16)
        vh = jnp.where((lane >= h * DA_DV) & (lane < (h + 1) * DA_DV), vall, jnp.zeros_like(vall))
        acc = acc + _dot(a, vh)
    gmat = _group_matrix(256, DA_DV, 1.0 / DA_DV)
    o_ref[...] = (_head_rms(acc, gmat, g) * (1.0 - lam_init)).astype(o_ref.dtype)


def _lam(lp):
    return (jnp.exp(jnp.sum(lp[0:1] * lp[1:2], axis=-1, keepdims=True))
            - jnp.exp(jnp.sum(lp[2:3] * lp[3:4], axis=-1, keepdims=True)))


def _attn_ctx_kernel(lp_ref, q_ref, k_ref, v_ref, g_ref, o_ref, *, lam_init):
    lam = _lam(lp_ref[...]) + lam_init
    q = q_ref[...] * (DA_DK ** -0.5)
    _attn_core(q, k_ref[...].astype(BF16), v_ref[...].astype(BF16), lam, g_ref[...], o_ref, lam_init)


def _attn_dec_kernel(lp_ref, q_ref, k_ref, v_ref, ck_ref, cv_ref, cosq_ref, sinq_ref, cosk_ref, sink_ref,
                     g_ref, o_ref, kall, vall, *, lam_init):
    n_lat = k_ref.shape[0]
    n_past = ck_ref.shape[1]

    @pl.when(pl.program_id(1) == 0)
    def _():
        kall[0:n_lat, :] = _rope(k_ref[...], cosk_ref[...], sink_ref[...]).astype(BF16)
        kall[n_lat:n_lat + n_past, :] = ck_ref[0].astype(BF16)
        vall[0:n_lat, :] = v_ref[...].astype(BF16)
        vall[n_lat:n_lat + n_past, :] = cv_ref[0].astype(BF16)

    lam = _lam(lp_ref[...]) + lam_init
    q = _rope(q_ref[...], cosq_ref[...], sinq_ref[...]) * (DA_DK ** -0.5)
    _attn_core(q, kall[...], vall[...], lam, g_ref[...], o_ref, lam_init)


def attention_ctx(u, lam_p, norm_g, layer, n_b, seq, row0):
    lam_init = 0.8 - 0.6 * math.exp(-0.3 * layer)
    rb = row0 // seq
    cq, ck, cv = U_DA // 256, U_DA // 256 + 1, U_DA // 256 + 2
    return pl.pallas_call(
        functools.partial(_attn_ctx_kernel, lam_init=lam_init),
        grid=(n_b,),
        in_specs=[pl.BlockSpec((4, DA_DK), lambda b: (0, 0)),
                  pl.BlockSpec((seq, 256), lambda b: (rb + b, cq)),
                  pl.BlockSpec((seq, 256), lambda b: (rb + b, ck)),
                  pl.BlockSpec((seq, 256), lambda b: (rb + b, cv)),
                  pl.BlockSpec((1, 256), lambda b: (0, 0))],
        out_specs=pl.BlockSpec((seq, 256), lambda b: (b, 0)),
        out_shape=jax.ShapeDtypeStruct((n_b * seq, 256), BF16),
        compiler_params=_cparams(("parallel",)),
    )(lam_p, u, u, u, norm_g.reshape(1, 256))


def attention_dec(u, cache_k, cache_v, lam_p, norm_g, layer, n_b, seq, row0, tq=256):
    lam_init = 0.8 - 0.6 * math.exp(-0.3 * layer)
    rb = row0 // seq
    rq = row0 // tq
    nq = seq // tq
    n_past = cache_k.shape[1]
    cq, ck, cv = U_DA // 256, U_DA // 256 + 1, U_DA // 256 + 2
    cos, sin = _rope_tables(seq)
    return pl.pallas_call(
        functools.partial(_attn_dec_kernel, lam_init=lam_init),
        grid=(n_b, nq),
        in_specs=[pl.BlockSpec((4, DA_DK), lambda b, i: (0, 0)),
                  pl.BlockSpec((tq, 256), lambda b, i: (rq + b * nq + i, cq)),
                  pl.BlockSpec((seq, 256), lambda b, i: (rb + b, ck)),
                  pl.BlockSpec((seq, 256), lambda b, i: (rb + b, cv)),
                  pl.BlockSpec((1, n_past, 256), lambda b, i: (b, 0, 0)),
                  pl.BlockSpec((1, n_past, 256), lambda b, i: (b, 0, 0)),
                  pl.BlockSpec((tq, 256), lambda b, i: (i, 0)),
                  pl.BlockSpec((tq, 256), lambda b, i: (i, 0)),
                  pl.BlockSpec((seq, 256), lambda b, i: (0, 0)),
                  pl.BlockSpec((seq, 256), lambda b, i: (0, 0)),
                  pl.BlockSpec((1, 256), lambda b, i: (0, 0))],
        out_specs=pl.BlockSpec((tq, 256), lambda b, i: (b * nq + i, 0)),
        out_shape=jax.ShapeDtypeStruct((n_b * seq, 256), BF16),
        scratch_shapes=[pltpu.VMEM((seq + n_past, 256), BF16), pltpu.VMEM((seq + n_past, 256), BF16)],
        compiler_params=_cparams(("parallel", "arbitrary")),
    )(lam_p, u, u, u, cache_k, cache_v, cos, sin, cos, sin, norm_g.reshape(1, 256))


def _conv_chunk(u_ref, cw, c, n_chunks, col0, ncol, base=0):
    end = base + n_chunks * CHUNK
    r0 = pl.multiple_of(base + c * CHUNK, CHUNK)
    x = u_ref[pl.ds(r0, CHUNK), col0:col0 + ncol]
    hp = u_ref[pl.ds(pl.multiple_of(jnp.maximum(r0 - 8, base), 8), 8), col0:col0 + ncol][7:8]
    hn = u_ref[pl.ds(pl.multiple_of(jnp.minimum(r0 + CHUNK, end - 8), 8), 8), col0:col0 + ncol][0:1]
    hp = jnp.where(c == 0, 0.0, hp)
    hn = jnp.where(c == n_chunks - 1, 0.0, hn)
    row = lax.broadcasted_iota(jnp.int32, (CHUNK, 1), 0)
    prev = jnp.where(row == 0, hp, pltpu.roll(x, 1, 0))
    nxt = jnp.where(row == CHUNK - 1, hn, pltpu.roll(x, CHUNK - 1, 0))
    return prev * cw[0:1] + x * cw[1:2] + nxt * cw[2:3]


def _cumsum_rows(x):
    row = lax.broadcasted_iota(jnp.int32, (x.shape[0], 1), 0)
    s = 1
    while s < x.shape[0]:
        x = x + jnp.where(row >= s, pltpu.roll(x, s, 0), 0.0)
        s *= 2
    return x


def _time_cumsum(x, reverse):
    p = _cumsum_rows(x)
    total = p[x.shape[0] - 1:x.shape[0]]
    if reverse:
        p = total - p + x
    return p, total


def _before_masks(reverse):
    t = lax.broadcasted_iota(jnp.int32, (CHUNK, CHUNK), 0)
    s = lax.broadcasted_iota(jnp.int32, (CHUNK, CHUNK), 1)
    if reverse:
        return s > t, s >= t
    return s < t, s <= t


def _mm(a, b):
    return _dot(a.astype(BF16), b.astype(BF16))


def _mm_nt(a, b):
    return _dot_nt(a.astype(BF16), b.astype(BF16))


def _mm_tn(a, b):
    return _dot_tn(a.astype(BF16), b.astype(BF16))


def _rwkv_chunk(uc, d, prm, s_ref):
    w0, w2, a0, a2, k_k, k_a, r_k = prm
    reverse = d == 1
    r = uc[:, 0:256]
    kb = uc[:, 256:512]
    v = uc[:, 512:768]
    wd = uc[:, 768:832]
    ad = uc[:, 832:896]
    ones_g = _group_matrix(RW_W, RW_N, 1.0)
    kk = kb * k_k
    kk = kk * lax.rsqrt(_dot_exact_rhs(kk * kk, ones_g) + EPS)
    w_pre = w0[d:d + 1] + _dot(jnp.tanh(wd), w2[d], HI)
    softplus_neg = jnp.maximum(-w_pre, 0.0) + jnp.log(1.0 + jnp.exp(-jnp.abs(w_pre)))
    logw = -jnp.exp(-softplus_neg - 0.5)
    a = _sigmoid(a0[d:d + 1] + _dot(ad, a2[d], HI))
    k = kb * (1.0 + (a - 1.0) * k_a)
    b = kk * a
    cum, total = _time_cumsum(logw, reverse)
    e_in = jnp.exp(cum)
    e_out = jnp.exp(-cum)
    kap = kk * jnp.exp(cum - logw)
    bet = b * e_out
    kt = k * e_out
    rt = r * e_in
    g_end = jnp.exp(total)
    n = RW_H * CHUNK
    row = lax.broadcasted_iota(jnp.int32, (n, n), 0)
    col = lax.broadcasted_iota(jnp.int32, (n, n), 1)
    same_head = (row // CHUNK) == (col // RW_N)
    t_row, t_col = row % CHUNK, col % CHUNK
    strict = (t_col > t_row) if reverse else (t_col < t_row)
    incl = (t_col >= t_row) if reverse else (t_col <= t_row)
    eye = (row == col).astype(F32)

    def bd(x):
        xb = x.astype(BF16)
        return jnp.where(same_head, jnp.concatenate([xb] * RW_H, axis=0), jnp.zeros((n, n), BF16))

    kap_b, rt_b, bet_b, kt_b, v_b = bd(kap), bd(rt), bd(bet), bd(kt), bd(v)
    x = jnp.concatenate([kap_b, rt_b], axis=0)
    gb = _dot_nt(x, bet_b)
    gk = _dot_nt(x, kt_b)
    a_b = jnp.where(strict, gb[:n], 0.0)
    m_b = jnp.where(incl, gb[n:], 0.0)
    a_k = jnp.where(strict, gk[:n], 0.0)
    m_k = jnp.where(incl, gk[n:], 0.0)
    npow = -a_b
    tinv = eye + npow
    for _ in range(5):
        npow = _mm(npow, npow)
        tinv = tinv + _mm(tinv, npow)
    s0 = s_ref[...]
    xs = _dot_nt(x, s0.astype(BF16))
    u = -_mm(tinv, xs[:n] + _dot(a_k.astype(BF16), v_b))
    uv = jnp.concatenate([u.astype(BF16), v_b], axis=0)
    y_bd = xs[n:] + _dot(jnp.concatenate([m_b, m_k], axis=1).astype(BF16), uv)
    s_ref[...] = (s0 + _dot_tn(uv, jnp.concatenate([bet_b, kt_b], axis=0))) * g_end
    y = y_bd[0:CHUNK]
    for h in range(1, RW_H):
        y = y + y_bd[h * CHUNK:(h + 1) * CHUNK]
    bonus = _dot_exact_rhs(r * k * r_k, ones_g) * v
    return y, bonus


def _rwkv_kernel(*refs, has_state):
    if has_state:
        (u_ref, cw_ref, w0_ref, w2_ref, a0_ref, a2_ref, g2_ref, vec_ref, s0_ref, o_ref, sf_ref) = refs[:11]
        scratch = refs[11:]
    else:
        (u_ref, cw_ref, w0_ref, w2_ref, a0_ref, a2_ref, g2_ref, vec_ref, o_ref, sf_ref) = refs[:10]
        scratch = refs[10:]
    s_refs, (y_refs, bonus_refs) = scratch[:-4], (scratch[-4:-2], scratch[-2:])
    n_seq = sf_ref.shape[0]
    seq = u_ref.shape[0] // n_seq
    n_chunks = seq // CHUNK
    for g in range(n_seq):
        for d in range(2):
            s_ref = s_refs[2 * g + d]
            s_ref[...] = jnp.zeros(s_ref.shape, F32)
            if has_state:
                for h in range(RW_H):
                    s_ref[h * RW_N:(h + 1) * RW_N, h * RW_N:(h + 1) * RW_N] = s0_ref[g, d, h]
    vec = vec_ref[...]
    prm = (w0_ref[...], w2_ref, a0_ref[...], a2_ref, vec[0:1], vec[1:2], vec[2:3])
    cw = cw_ref[...]

    def step(j, carry):
        for g in range(n_seq):
            for d in range(2):
                c = j if d == 0 else n_chunks - 1 - j
                uc = _conv_chunk(u_ref, cw, c, n_chunks, 0, 1024, base=g * seq)
                y, bonus = _rwkv_chunk(uc, d, prm, s_refs[2 * g + d])
                rows = pl.ds(pl.multiple_of(g * seq + c * CHUNK, CHUNK), CHUNK)
                y_refs[d][rows, :] = y
                bonus_refs[d][rows, :] = bonus
        return carry

    lax.fori_loop(0, n_chunks, step, 0)
    for g in range(n_seq):
        for d in range(2):
            for h in range(RW_H):
                sf_ref[g, d, h] = s_refs[2 * g + d][h * RW_N:(h + 1) * RW_N, h * RW_N:(h + 1) * RW_N]

    mean_g = _group_matrix(RW_W, RW_N, 1.0 / RW_N)

    def finish(c, carry):
        for g in range(n_seq):
            rows = pl.ds(pl.multiple_of(g * seq + c * CHUNK, CHUNK), CHUNK)
            gd = _conv_chunk(u_ref, cw[:, 896:1024], c, n_chunks, 896, 128, base=g * seq)
            gate = _mm(_sigmoid(gd), g2_ref[...])
            y = y_refs[0][rows, :] + y_refs[1][rows, :]
            bonus = bonus_refs[0][rows, :] + bonus_refs[1][rows, :]
            yc = y - _dot_exact_rhs(y, mean_g)
            o_ref[rows, :] = ((_head_rms(yc, mean_g, vec[3:4]) + bonus) * gate).astype(o_ref.dtype)
        return carry

    lax.fori_loop(0, n_chunks, finish, 0)


def _transpose_small(x):
    eye = (lax.broadcasted_iota(jnp.int32, (CHUNK, CHUNK), 0)
           == lax.broadcasted_iota(jnp.int32, (CHUNK, CHUNK), 1)).astype(BF16)
    hi, mid, lo = _split3(x)
    return _dot_tn(hi, eye) + _dot_tn(mid, eye) + _dot_tn(lo, eye)


def _mlstm_chunk(qk, v, small, d, bias, c_ref, n_ref, m_ref):
    reverse = d == 1
    lane = lax.broadcasted_iota(jnp.int32, (1, LANES), 1)
    pre = small + bias
    lf = jnp.where((lane >= 8) & (lane < 16), _log_sigmoid(pre), 0.0)
    fcum, ftot = _time_cumsum(lf, reverse)
    w = jnp.where((lane >= 8) & (lane < 16), fcum, pre)
    wt = _transpose_small(w)
    n = ML_H * CHUNK
    heads = [d * ML_H + h for h in range(ML_H)]
    rows_of = lambda pieces: jnp.concatenate(pieces, axis=0)
    cols_of = lambda pieces: jnp.concatenate(pieces, axis=1)
    f_c = rows_of([w[:, 8 + j:9 + j] for j in heads])
    i_c = rows_of([w[:, j:j + 1] for j in heads])
    f_r = cols_of([wt[8 + j:9 + j, :] for j in heads])
    i_r = cols_of([wt[j:j + 1, :] for j in heads])
    f_l = [ftot[:, 8 + j:9 + j] for j in heads]
    m_prev = [m_ref[j:j + 1, 0:1] for j in heads]
    col1 = lambda vals: rows_of([jnp.broadcast_to(x, (CHUNK, 1)) for x in vals])
    row = lax.broadcasted_iota(jnp.int32, (n, n), 0)
    col = lax.broadcasted_iota(jnp.int32, (n, n), 1)
    t_row, t_col = row % CHUNK, col % CHUNK
    incl = ((row // CHUNK) == (col // CHUNK)) & ((t_col >= t_row) if reverse else (t_col <= t_row))
    row_k = lax.broadcasted_iota(jnp.int32, (n, ML_H * ML_DK), 0)
    col_k = lax.broadcasted_iota(jnp.int32, (n, ML_H * ML_DK), 1)
    head_k = (row_k // CHUNK) == (col_k // ML_DK)
    head_v = (row // CHUNK) == (col // ML_DV)
    q_bd = jnp.where(head_k, jnp.concatenate([qk[:, 0:128]] * ML_H, axis=0), 0.0)
    k_bd = jnp.where(head_k, jnp.concatenate([qk[:, 128:256] * (ML_DK ** -0.5)] * ML_H, axis=0), 0.0)
    v_bd = jnp.where(head_v, jnp.concatenate([v] * ML_H, axis=0), 0.0)
    c_prev = c_ref[d]
    n_prev = cols_of([n_ref[j:j + 1, :] for j in heads])
    dm = jnp.where(incl, f_c - f_r + i_r, -jnp.inf)
    inter = f_c + col1(m_prev)
    mt = jnp.maximum(inter, jnp.max(dm, axis=-1, keepdims=True))
    wi = jnp.exp(dm - mt)
    we = jnp.exp(inter - mt)
    q_b = q_bd.astype(BF16)
    k_b = k_bd.astype(BF16)
    s = _dot_nt(q_b, k_b) * wi
    num = we * _dot_nt(q_b, c_prev.astype(BF16)) + _mm(s, v_bd)
    den = we * jnp.sum(q_bd * n_prev, axis=-1, keepdims=True) + jnp.sum(s, axis=-1, keepdims=True)
    h_bd = num / jnp.maximum(jnp.abs(den), jnp.exp(-mt))
    out = h_bd[0:CHUNK]
    for h in range(1, ML_H):
        out = out + h_bd[h * CHUNK:(h + 1) * CHUNK]
    gs_c = col1(f_l) - f_c + i_c
    m_new = [jnp.maximum(f_l[h] + m_prev[h],
                         jnp.max(gs_c[h * CHUNK:(h + 1) * CHUNK], axis=0, keepdims=True)) for h in range(ML_H)]
    ws = jnp.exp(gs_c - col1(m_new))
    wc = [jnp.exp(f_l[h] + m_prev[h] - m_new[h]) for h in range(ML_H)]
    wc_rows = rows_of([jnp.broadcast_to(x, (ML_DV, 1)) for x in wc])
    wc_cols = cols_of([jnp.broadcast_to(x, (1, ML_DK)) for x in wc])
    c_ref[d] = wc_rows * c_prev + _dot_tn((v_bd * ws).astype(BF16), k_b)
    n_new = wc_cols * n_prev + jnp.sum(k_bd * ws, axis=0, keepdims=True)
    for h, j in enumerate(heads):
        n_ref[j:j + 1, :] = n_new[:, h * ML_DK:(h + 1) * ML_DK]
        m_ref[j:j + 1, :] = jnp.broadcast_to(m_new[h], (1, LANES))
    return out


def _mlstm_kernel(*refs, has_state):
    if has_state:
        (qk_ref, v_ref, og_ref, sm_ref, cw_ref, bias_ref, g_ref, c0_ref, n0_ref, m0_ref,
         o_ref, cf_ref, nf_ref, mf_ref, c_ref, n_ref, m_ref, h_ref) = refs
        c_ref[...] = jnp.zeros(c_ref.shape, F32)
        for d in range(2):
            for h in range(ML_H):
                c_ref[d, h * ML_DV:(h + 1) * ML_DV, h * ML_DK:(h + 1) * ML_DK] = c0_ref[0, d, h]
        n_ref[...] = n0_ref[0]
        m_ref[...] = m0_ref[0]
    else:
        (qk_ref, v_ref, og_ref, sm_ref, cw_ref, bias_ref, g_ref,
         o_ref, cf_ref, nf_ref, mf_ref, c_ref, n_ref, m_ref, h_ref) = refs
        c_ref[...] = jnp.zeros(c_ref.shape, F32)
        n_ref[...] = jnp.zeros(n_ref.shape, F32)
        m_ref[...] = jnp.zeros(m_ref.shape, F32)
    n_chunks = qk_ref.shape[0] // CHUNK
    h_ref[...] = jnp.zeros(h_ref.shape, F32)
    cw = cw_ref[...]
    bias = bias_ref[...]

    def step(j, carry):
        for d in range(2):
            c = j if d == 0 else n_chunks - 1 - j
            rows = pl.ds(pl.multiple_of(c * CHUNK, CHUNK), CHUNK)
            qk = _conv_chunk(qk_ref, cw, c, n_chunks, 0, 256)
            h_ref[rows, :] += _mlstm_chunk(qk, v_ref[rows, :], sm_ref[rows, :], d, bias,
                                           c_ref, n_ref, m_ref)
        return carry

    lax.fori_loop(0, n_chunks, step, 0)
    for d in range(2):
        for h in range(ML_H):
            cf_ref[0, d, h] = c_ref[d, h * ML_DV:(h + 1) * ML_DV, h * ML_DK:(h + 1) * ML_DK]
    nf_ref[0] = n_ref[...]
    mf_ref[0] = m_ref[...]
    gmat = _group_matrix(ML_H * ML_DV, ML_DV, 1.0 / ML_DV)

    def finish(c, carry):
        rows = pl.ds(pl.multiple_of(c * CHUNK, CHUNK), CHUNK)
        o_ref[rows, :] = (_head_rms(h_ref[rows, :], gmat, g_ref[...])
                          * _sigmoid(og_ref[rows, :])).astype(o_ref.dtype)
        return carry

    lax.fori_loop(0, n_chunks, finish, 0)


def mlstm_branch(u, conv_w, p, layer, n_b, seq, row0, state0=None):
    i = layer
    rb = row0 // seq
    has_state = state0 is not None
    bias = jnp.zeros((1, LANES), F32)
    bias = bias.at[0, 0:8].set(p['ml_i_bias'][i].reshape(8)).at[0, 8:16].set(p['ml_f_bias'][i].reshape(8))
    full = lambda *shape: pl.BlockSpec(shape, lambda b: (0,) * len(shape))
    in_specs = [pl.BlockSpec((seq, 256), lambda b: (rb + b, U_MLQK // 256)),
                pl.BlockSpec((seq, 256), lambda b: (rb + b, U_MLVO // 256)),
                pl.BlockSpec((seq, 256), lambda b: (rb + b, U_MLVO // 256 + 1)),
                pl.BlockSpec((seq, LANES), lambda b: (rb + b, U_SMALL // LANES)),
                full(3, 256), full(1, LANES), full(1, 256)]
    args = [u, u, u, u, conv_w[:, U_MLQK:U_MLQK + 256], bias, p['ml_norm_g'][i].reshape(1, 256)]
    if has_state:
        c0, n0, m0 = state0
        in_specs += [pl.BlockSpec((1, 2, ML_H, ML_DV, ML_DK), lambda b: (b, 0, 0, 0, 0)),
                     pl.BlockSpec((1, 8, ML_DK), lambda b: (b, 0, 0)),
                     pl.BlockSpec((1, 8, LANES), lambda b: (b, 0, 0))]
        args += [c0, n0.reshape(n_b, 8, ML_DK),
                 jnp.broadcast_to(m0.reshape(n_b, 8, 1), (n_b, 8, LANES))]
    o, cf, nf, mf = pl.pallas_call(
        functools.partial(_mlstm_kernel, has_state=has_state),
        grid=(n_b,),
        in_specs=in_specs,
        out_specs=[pl.BlockSpec((seq, 256), lambda b: (b, 0)),
                   pl.BlockSpec((1, 2, ML_H, ML_DV, ML_DK), lambda b: (b, 0, 0, 0, 0)),
                   pl.BlockSpec((1, 8, ML_DK), lambda b: (b, 0, 0)),
                   pl.BlockSpec((1, 8, LANES), lambda b: (b, 0, 0))],
        out_shape=[jax.ShapeDtypeStruct((n_b * seq, 256), BF16),
                   jax.ShapeDtypeStruct((n_b, 2, ML_H, ML_DV, ML_DK), F32),
                   jax.ShapeDtypeStruct((n_b, 8, ML_DK), F32),
                   jax.ShapeDtypeStruct((n_b, 8, LANES), F32)],
        scratch_shapes=[pltpu.VMEM((2, ML_H * ML_DV, ML_H * ML_DK), F32), pltpu.VMEM((8, ML_DK), F32),
                        pltpu.VMEM((8, LANES), F32), pltpu.VMEM((seq, 256), F32)],
        compiler_params=_cparams(("parallel",)),
    )(*args)
    return o, cf, nf.reshape(n_b, 2, ML_H, ML_DK), mf[:, :, 0].reshape(n_b, 2, ML_H)


def _gla_chunk(qk, v, small, d, a2_ref, abias, st_ref):
    reverse = d == 1
    q = qk[:, 0:128] * (GL_DK ** -0.5)
    k = qk[:, 128:256]
    la = _log_sigmoid(_dot(small, a2_ref[d], HI) + abias[d:d + 1]) * (1.0 / GL_TAU)
    cum, total = _time_cumsum(la, reverse)
    mid = cum[CHUNK // 2:CHUNK // 2 + 1]
    qe = q * jnp.exp(cum - mid)
    ke = k * jnp.exp(mid - cum)
    qg = q * jnp.exp(cum)
    kg = k * jnp.exp(total - cum)
    g_end = jnp.exp(total)
    n = GL_H * CHUNK
    row = lax.broadcasted_iota(jnp.int32, (n, n), 0)
    col = lax.broadcasted_iota(jnp.int32, (n, n), 1)
    t_row, t_col = row % CHUNK, col % CHUNK
    incl = ((row // CHUNK) == (col // CHUNK)) & ((t_col >= t_row) if reverse else (t_col <= t_row))
    row_k = lax.broadcasted_iota(jnp.int32, (n, GL_H * GL_DK), 0)
    col_k = lax.broadcasted_iota(jnp.int32, (n, GL_H * GL_DK), 1)
    head_k = (row_k // CHUNK) == (col_k // GL_DK)
    head_v = (row // CHUNK) == (col // GL_DV)

    def bd(x, mask):
        xb = x.astype(BF16)
        return jnp.where(mask, jnp.concatenate([xb] * GL_H, axis=0), jnp.zeros(mask.shape, BF16))

    v_bd = bd(v, head_v)
    a = jnp.where(incl, _dot_nt(bd(qe, head_k), bd(ke, head_k)), 0.0)
    st = st_ref[d]
    o_bd = _dot_nt(bd(qg, head_k), st.astype(BF16)) + _dot(a.astype(BF16), v_bd)
    st_ref[d] = st * g_end + _dot_tn(v_bd, bd(kg, head_k))
    out = o_bd[0:CHUNK]
    for h in range(1, GL_H):
        out = out + o_bd[h * CHUNK:(h + 1) * CHUNK]
    return out


def _gla_kernel(*refs, has_state):
    if has_state:
        (qk_ref, v_ref, gg_ref, sm_ref, cw_ref, a2_ref, ab_ref, g_ref, s0_ref,
         o_ref, sf_ref, st_ref, acc_ref) = refs
        st_ref[...] = jnp.zeros(st_ref.shape, F32)
        for d in range(2):
            for h in range(GL_H):
                st_ref[d, h * GL_DV:(h + 1) * GL_DV, h * GL_DK:(h + 1) * GL_DK] = s0_ref[0, d, h]
    else:
        (qk_ref, v_ref, gg_ref, sm_ref, cw_ref, a2_ref, ab_ref, g_ref,
         o_ref, sf_ref, st_ref, acc_ref) = refs
        st_ref[...] = jnp.zeros(st_ref.shape, F32)
    n_chunks = qk_ref.shape[0] // CHUNK
    acc_ref[...] = jnp.zeros(acc_ref.shape, F32)
    cw = cw_ref[...]
    abias = ab_ref[...]

    def step(j, carry):
        for d in range(2):
            c = j if d == 0 else n_chunks - 1 - j
            rows = pl.ds(pl.multiple_of(c * CHUNK, CHUNK), CHUNK)
            qk = _conv_chunk(qk_ref, cw, c, n_chunks, 0, 256)
            acc_ref[rows, :] += _gla_chunk(qk, v_ref[rows, :], sm_ref[rows, :], d, a2_ref, abias, st_ref)
        return carry

    lax.fori_loop(0, n_chunks, step, 0)
    for d in range(2):
        for h in range(GL_H):
            sf_ref[0, d, h] = st_ref[d, h * GL_DV:(h + 1) * GL_DV, h * GL_DK:(h + 1) * GL_DK]
    gmat = _group_matrix(GL_H * GL_DV, GL_DV, 1.0 / GL_DV)

    def finish(c, carry):
        rows = pl.ds(pl.multiple_of(c * CHUNK, CHUNK), CHUNK)
        gg = gg_ref[rows, :]
        o_ref[rows, :] = (_head_rms(acc_ref[rows, :], gmat, g_ref[...]) * (gg * _sigmoid(gg))).astype(o_ref.dtype)
        return carry

    lax.fori_loop(0, n_chunks, finish, 0)


def gla_branch(u, conv_w, p, layer, n_b, seq, row0, state0=None):
    i = layer
    rb = row0 // seq
    has_state = state0 is not None
    a2 = jnp.zeros((2, LANES, GL_H * GL_DK), F32).at[:, 16:32, :].set(p['gl_a2'][i])
    full = lambda *shape: pl.BlockSpec(shape, lambda b: (0,) * len(shape))
    in_specs = [pl.BlockSpec((seq, 256), lambda b: (rb + b, U_GLQK // 256)),
                pl.BlockSpec((seq, 256), lambda b: (rb + b, U_GLVG // 256)),
                pl.BlockSpec((seq, 256), lambda b: (rb + b, U_GLVG // 256 + 1)),
                pl.BlockSpec((seq, LANES), lambda b: (rb + b, U_SMALL // LANES)),
                full(3, 256), full(2, LANES, GL_H * GL_DK), full(2, GL_H * GL_DK), full(1, 256)]
    args = [u, u, u, u, conv_w[:, U_GLQK:U_GLQK + 256], a2, p['gl_a_bias'][i],
            p['gl_norm_g'][i].reshape(1, 256)]
    if has_state:
        in_specs.append(pl.BlockSpec((1, 2, GL_H, GL_DV, GL_DK), lambda b: (b, 0, 0, 0, 0)))
        args.append(jnp.swapaxes(state0, -1, -2))
    o, sf = pl.pallas_call(
        functools.partial(_gla_kernel, has_state=has_state),
        grid=(n_b,),
        in_specs=in_specs,
        out_specs=[pl.BlockSpec((seq, 256), lambda b: (b, 0)),
                   pl.BlockSpec((1, 2, GL_H, GL_DV, GL_DK), lambda b: (b, 0, 0, 0, 0))],
        out_shape=[jax.ShapeDtypeStruct((n_b * seq, 256), BF16),
                   jax.ShapeDtypeStruct((n_b, 2, GL_H, GL_DV, GL_DK), F32)],
        scratch_shapes=[pltpu.VMEM((2, GL_H * GL_DV, GL_H * GL_DK), F32), pltpu.VMEM((seq, 256), F32)],
        compiler_params=_cparams(("parallel",)),
    )(*args)
    return o, jnp.swapaxes(sf, -1, -2)


RW_SEQS_PER_STEP = 2


def rwkv_branch(u, conv_w, p, layer, n_b, seq, row0, state0=None):
    i = layer
    g = RW_SEQS_PER_STEP
    rows = g * seq
    rb = row0 // rows
    vec = jnp.stack([p['rw_k_k'][i], p['rw_k_a'][i], p['rw_r_k'][i], p['rw_norm_g'][i]])
    has_state = state0 is not None
    full = lambda *shape: pl.BlockSpec(shape, lambda b: (0,) * len(shape))
    in_specs = [pl.BlockSpec((rows, 1024), lambda b: (rb + b, 0), pipeline_mode=pl.Buffered(1)),
                full(3, 1024), full(2, RW_W), full(2, 64, RW_W), full(2, RW_W), full(2, 64, RW_W),
                full(128, RW_W), full(4, RW_W)]
    args = [u, conv_w[:, 0:1024], p['rw_w0'][i], p['rw_w2'][i], p['rw_a0'][i], p['rw_a2'][i],
            p['rw_g2'][i], vec]
    if has_state:
        in_specs.append(pl.BlockSpec((g, 2, RW_H, RW_N, RW_N), lambda b: (b, 0, 0, 0, 0)))
        args.append(state0)
    return pl.pallas_call(
        functools.partial(_rwkv_kernel, has_state=has_state),
        grid=(n_b // g,),
        in_specs=in_specs,
        out_specs=[pl.BlockSpec((rows, RW_W), lambda b: (b, 0)),
                   pl.BlockSpec((g, 2, RW_H, RW_N, RW_N), lambda b: (b, 0, 0, 0, 0))],
        out_shape=[jax.ShapeDtypeStruct((n_b * seq, RW_W), BF16),
                   jax.ShapeDtypeStruct((n_b, 2, RW_H, RW_N, RW_N), F32)],
        scratch_shapes=([pltpu.VMEM((RW_W, RW_W), F32)] * (2 * g) + [pltpu.VMEM((rows, RW_W), F32)] * 4),
        compiler_params=_cparams(("parallel",)),
    )(*args)


TOKEN_TILE = 512


def kernel(x_prompt, x_sample, cache_attn_k, cache_attn_v, state_rwkv, state_mlstm_c, state_mlstm_n,
           state_mlstm_m, state_gla, c, c_ctx, norm1_g, norm2_g, final_g, w_ada, b_ada, w_in, conv_w,
           da_lam_q1, da_lam_k1, da_lam_q2, da_lam_k2, da_norm_g, rw_w0, rw_w2, rw_a0, rw_a2, rw_g2,
           rw_k_k, rw_k_a, rw_r_k, rw_norm_g, ml_i_bias, ml_f_bias, ml_norm_g, gl_a2, gl_a_bias, gl_norm_g,
           w_branch, w_bgate, w_out, w_router, b_router, w_gu, b_gu, w_dn, b_dn):
    p = dict(rw_w0=rw_w0, rw_w2=rw_w2, rw_a0=rw_a0, rw_a2=rw_a2, rw_g2=rw_g2, rw_k_k=rw_k_k, rw_k_a=rw_k_a,
             rw_r_k=rw_r_k, rw_norm_g=rw_norm_g, ml_i_bias=ml_i_bias, ml_f_bias=ml_f_bias,
             ml_norm_g=ml_norm_g, gl_a2=gl_a2, gl_a_bias=gl_a_bias, gl_norm_g=gl_norm_g)
    n_cb, n_cl, d = x_prompt.shape
    n_db, n_dl, _ = x_sample.shape
    depth = w_in.shape[0]
    n_ctx = n_cb * n_cl
    n_tok = n_ctx + n_db * n_dl
    tm = TOKEN_TILE
    ctx_row = n_db

    def mod_index(i):
        return jnp.where(i < n_ctx // tm, ctx_row, (i - n_ctx // tm) * tm // n_dl)

    n_mod = 16
    cvecs = jnp.zeros((n_mod, d), F32).at[:n_db].set(c).at[ctx_row].set(c_ctx)
    mods = adaln_all(cvecs, w_ada, b_ada)
    mods = jnp.pad(mods.reshape(depth, n_mod, 6, d), ((0, 0), (0, 0), (0, 2), (0, 0)))

    x = jnp.concatenate([x_prompt.reshape(n_ctx, d), x_sample.reshape(n_db * n_dl, d)], axis=0)
    states = []
    for l in range(depth):
        mod = mods[l]
        u, h = in_projection(x, mod, mod_index, norm1_g[l], _permute_w_in(w_in[l]).astype(BF16), tm)
        lam_p = jnp.stack([da_lam_q1[l], da_lam_k1[l], da_lam_q2[l], da_lam_k2[l]])
        n_past = cache_attn_k.shape[2]
        o_da = (attention_ctx(u, lam_p, da_norm_g[l], l, n_cb, n_cl, 0),
                attention_dec(u, cache_attn_k[:, l].reshape(n_db, n_past, 256),
                              cache_attn_v[:, l].reshape(n_db, n_past, 256), lam_p, da_norm_g[l], l,
                              n_db, n_dl, n_ctx))
        o_rw_c, rw_s = rwkv_branch(u, conv_w[l], p, l, n_cb, n_cl, 0)
        o_rw_d, _ = rwkv_branch(u, conv_w[l], p, l, n_db, n_dl, n_ctx, state_rwkv[:, l])
        o_ml_c, ml_c, ml_n, ml_m = mlstm_branch(u, conv_w[l], p, l, n_cb, n_cl, 0)
        o_ml_d = mlstm_branch(u, conv_w[l], p, l, n_db, n_dl, n_ctx,
                              (state_mlstm_c[:, l], state_mlstm_n[:, l], state_mlstm_m[:, l]))[0]
        o_gl_c, gl_s = gla_branch(u, conv_w[l], p, l, n_cb, n_cl, 0)
        o_gl_d, _ = gla_branch(u, conv_w[l], p, l, n_db, n_dl, n_ctx, state_gla[:, l])
        outs_ctx = [o_da[0], o_rw_c, o_ml_c, o_gl_c]
        outs_dec = [o_da[1], o_rw_d, o_ml_d, o_gl_d]
        new_k = u[:n_ctx, U_DA + 256:U_DA + 512].reshape(n_cb, n_cl, DA_H, 2, DA_DK)
        new_v = u[:n_ctx, U_DA + 512:U_DA + 768].reshape(n_cb, n_cl, DA_H, DA_DV)
        states.append((new_k, new_v, rw_s, ml_c, ml_n, ml_m, gl_s))

        x1, h3, route, tile_counts = merge_and_route(x, h, outs_ctx, outs_dec, mod, mod_index,
                                                     w_branch[l].astype(BF16), w_bgate[l].astype(BF16),
                                                     w_out[l].astype(BF16), norm2_g[l], w_router[l],
                                                     b_router[l], tm)
        block_e, row_slot, n_act, row_gate = _route_metadata(route, tile_counts, n_tok)
        ys3 = moe_experts(h3, block_e, row_slot, n_act, row_gate, w_gu[l], b_gu[l], w_dn[l], b_dn[l],
                          n_tok * TOP_K)
        x = moe_combine(ys3, x1, mod, mod_index, final_g, tm, final=(l == depth - 1))

    y_prompt = x[:n_ctx].reshape(n_cb, n_cl, d)
    y_sample = x[n_ctx:].reshape(n_db, n_dl, d)
    stacked = tuple(jnp.stack([s[j] for s in states], axis=1) for j in range(7))
    return (y_prompt, y_sample) + stacked
```

```python
import functools
import math

import numpy as np
import jax
import jax.numpy as jnp
from jax import lax
from jax.experimental import pallas as pl
from jax.experimental.pallas import tpu as pltpu

F32 = jnp.float32
BF16 = jnp.bfloat16

D_MODEL = 1024
GRID_W = 64
N_BRANCH = 4
BRANCH_W = 256
DA_H, DA_DK, DA_DV = 4, 32, 64
RW_H, RW_N = 4, 64
RW_W = RW_H * RW_N
ML_H, ML_DK, ML_DV = 4, 32, 64
GL_H, GL_DK, GL_DV = 4, 32, 64
GL_TAU = 16.0
CHUNK = 64
N_EXPERTS = 32
TOP_K = 4
D_FF = 1024
SWIGLU_ALPHA = 1.702
SWIGLU_LIMIT = 7.0
ROPE_BASE = 10000.0
EPS = 1e-6

LANES = 128
VMEM_LIMIT = 56 * 1024 * 1024

U_RW = 0
U_MLQK = 1024
U_GLQK = 1280
N_CONV = 1536
U_DA = 1536
U_MLVO = 2304
U_GLVG = 2816
U_SMALL = 3328
N_U = 3456

_REF_SPLITS = (('rw_r', 256), ('rw_k', 256), ('rw_v', 256), ('rw_wd', 64), ('rw_ad', 64), ('rw_gd', 128),
               ('ml_q', 128), ('ml_k', 128), ('gl_q', 128), ('gl_k', 128),
               ('da_q', 256), ('da_k', 256), ('da_v', 256), ('ml_v', 256), ('ml_o', 256), ('ml_i', 8),
               ('ml_f', 8), ('gl_v', 256), ('gl_a', 16), ('gl_g', 256))
_REF_OFF = {}
_o = 0
for _n, _s in _REF_SPLITS:
    _REF_OFF[_n] = (_o, _s)
    _o += _s
_MY_ORDER = ('rw_r', 'rw_k', 'rw_v', 'rw_wd', 'rw_ad', 'rw_gd', 'ml_q', 'ml_k', 'gl_q', 'gl_k',
             'da_q', 'da_k', 'da_v', 'ml_v', 'ml_o', 'gl_v', 'gl_g', 'ml_i', 'ml_f', 'gl_a')


def _permute_w_in(w):
    cols = [w[:, _REF_OFF[n][0]:_REF_OFF[n][0] + _REF_OFF[n][1]] for n in _MY_ORDER]
    cols.append(jnp.zeros((w.shape[0], N_U - sum(_REF_OFF[n][1] for n in _MY_ORDER)), w.dtype))
    return jnp.concatenate(cols, axis=1)


def _cparams(sem):
    return pltpu.CompilerParams(dimension_semantics=sem, vmem_limit_bytes=VMEM_LIMIT)


def _sigmoid(x):
    return 1.0 / (1.0 + jnp.exp(-x))


def _log_sigmoid(x):
    return jnp.minimum(x, 0.0) - jnp.log(1.0 + jnp.exp(-jnp.abs(x)))


def _dot(a, b, prec=None):
    return jnp.dot(a, b, preferred_element_type=F32, precision=prec)


def _dot_nt(a, b, prec=None):
    return lax.dot_general(a, b, (((1,), (1,)), ((), ())), preferred_element_type=F32, precision=prec)


def _dot_tn(a, b, prec=None):
    return lax.dot_general(a, b, (((0,), (0,)), ((), ())), preferred_element_type=F32, precision=prec)


HI = lax.Precision.HIGHEST


def _adaln_kernel(c_ref, w_ref, b_ref, o_ref):
    c = c_ref[...]
    s = c * _sigmoid(c)
    o_ref[0] = _dot(s, w_ref[0], HI) + b_ref[0]


def adaln_all(cvecs, w_ada, b_ada):
    depth, d, n = w_ada.shape
    r = cvecs.shape[0]
    tn = 1536
    return pl.pallas_call(
        _adaln_kernel,
        grid=(depth, n // tn),
        in_specs=[pl.BlockSpec((r, d), lambda l, j: (0, 0)),
                  pl.BlockSpec((1, d, tn), lambda l, j: (l, 0, j)),
                  pl.BlockSpec((1, 1, tn), lambda l, j: (l, 0, j))],
        out_specs=pl.BlockSpec((1, r, tn), lambda l, j: (l, 0, j)),
        out_shape=jax.ShapeDtypeStruct((depth, r, n), F32),
        compiler_params=_cparams(("arbitrary", "arbitrary")),
    )(cvecs, w_ada, b_ada.reshape(depth, 1, n))


def _rms(x, g):
    return x * lax.rsqrt(jnp.mean(x * x, axis=-1, keepdims=True) + EPS) * g


def _inproj_kernel(x_ref, mod_ref, g_ref, w_ref, u_ref, h_ref):
    x = x_ref[...]
    mod = mod_ref[0]
    h = _rms(x, g_ref[...]) * (1.0 + mod[1:2]) + mod[0:1]
    hb = h.astype(BF16)
    h_ref[...] = hb
    u_ref[...] = _dot(hb, w_ref[...])


def in_projection(x, mod, mod_index, norm_g, w_in_b, tm):
    t, d = x.shape
    return pl.pallas_call(
        _inproj_kernel,
        grid=(t // tm,),
        in_specs=[pl.BlockSpec((tm, d), lambda i: (i, 0)),
                  pl.BlockSpec((1, 8, d), lambda i: (mod_index(i), 0, 0)),
                  pl.BlockSpec((1, d), lambda i: (0, 0)),
                  pl.BlockSpec((d, N_U), lambda i: (0, 0))],
        out_specs=[pl.BlockSpec((tm, N_U), lambda i: (i, 0)),
                   pl.BlockSpec((tm, d), lambda i: (i, 0))],
        out_shape=[jax.ShapeDtypeStruct((t, N_U), F32), jax.ShapeDtypeStruct((t, d), BF16)],
        compiler_params=_cparams(("parallel",)),
    )(x, mod, norm_g.reshape(1, d), w_in_b)


ROUTE_W = LANES


def _to_tiles(o_ref, y):
    for s in range(y.shape[1] // LANES):
        o_ref[:, s, :] = y[:, s * LANES:(s + 1) * LANES]


def _from_tiles(x3):
    return jnp.concatenate([x3[:, s, :] for s in range(x3.shape[1])], axis=1)


def _merge_kernel(x_ref, h_ref, oda_c, oda_d, orw_c, orw_d, oml_c, oml_d, ogl_c, ogl_d, mod_ref, wbr_ref, wbg_ref,
                  wout_ref, g2_ref, wr_ref, br_ref, x1_ref, h2_ref, route_ref, cnt_ref, *, n_ctx_tiles):
    d = x_ref.shape[1]
    h = h_ref[...]
    mod = mod_ref[0]
    is_ctx = pl.program_id(0) < n_ctx_tiles
    merged = jnp.zeros(x_ref.shape, F32)
    for gi, (oc_ref, od_ref) in enumerate(((oda_c, oda_d), (orw_c, orw_d), (oml_c, oml_d), (ogl_c, ogl_d))):
        gate = _sigmoid(_dot(h, wbg_ref[:, gi * d:(gi + 1) * d]))
        o = jnp.where(is_ctx, oc_ref[...], od_ref[...])
        merged = merged + gate * _dot(o, wbr_ref[gi])
    y = _dot(merged.astype(BF16), wout_ref[...])
    x1 = x_ref[...] + mod[2:3] * y
    x1_ref[...] = x1
    h2 = _rms(x1, g2_ref[...]) * (1.0 + mod[4:5]) + mod[3:4]
    _to_tiles(h2_ref, h2)
    lane = lax.broadcasted_iota(jnp.int32, (1, ROUTE_W), 1)
    logits = jnp.where(lane < N_EXPERTS, _dot(h2.astype(BF16), wr_ref[...]) + br_ref[...], -jnp.inf)
    vals, ids = [], []
    for _ in range(TOP_K):
        m = jnp.max(logits, axis=-1, keepdims=True)
        idx = jnp.min(jnp.where(logits == m, lane, ROUTE_W), axis=-1, keepdims=True)
        vals.append(m)
        ids.append(idx)
        logits = jnp.where(lane == idx, -jnp.inf, logits)
    es = [jnp.exp(v - vals[0]) for v in vals]
    inv = 1.0 / sum(es)
    route = jnp.zeros((x_ref.shape[0], ROUTE_W), F32)
    for r in range(TOP_K):
        route = jnp.where(lane == r, es[r] * inv, route)
        route = jnp.where(lane == TOP_K + r, ids[r].astype(F32), route)
    route_ref[...] = route
    hist = sum(jnp.sum((lane == ids[r]).astype(F32), axis=0, keepdims=True) for r in range(TOP_K))
    cnt_ref[0] = jnp.broadcast_to(hist, cnt_ref.shape[1:])


def merge_and_route(x, h, outs_ctx, outs_dec, mod, mod_index, w_branch_b, w_bgate_b, w_out_b, norm2_g, w_router,
                    b_router, tm):
    t, d = x.shape
    nct = outs_ctx[0].shape[0] // tm
    ctx_row = pl.BlockSpec((tm, BRANCH_W), lambda i: (jnp.minimum(i, nct - 1), 0))
    dec_row = pl.BlockSpec((tm, BRANCH_W), lambda i: (jnp.maximum(i - nct, 0), 0))
    wr = jnp.zeros((d, ROUTE_W), BF16).at[:, :N_EXPERTS].set(w_router.astype(BF16))
    br = jnp.zeros((1, ROUTE_W), F32).at[0, :N_EXPERTS].set(b_router)
    row = lambda w: pl.BlockSpec((tm, w), lambda i: (i, 0))
    full = lambda *shape: pl.BlockSpec(shape, lambda i: (0,) * len(shape))
    return pl.pallas_call(
        functools.partial(_merge_kernel, n_ctx_tiles=nct),
        grid=(t // tm,),
        in_specs=[row(d), row(d)] + [ctx_row, dec_row] * N_BRANCH + [
                  pl.BlockSpec((1, 8, d), lambda i: (mod_index(i), 0, 0)),
                  full(N_BRANCH, BRANCH_W, d), full(d, N_BRANCH * d), full(d, d), full(1, d),
                  full(d, ROUTE_W), full(1, ROUTE_W)],
        out_specs=[row(d), pl.BlockSpec((tm, d // LANES, LANES), lambda i: (i, 0, 0)), row(ROUTE_W),
                   pl.BlockSpec((1, 8, ROUTE_W), lambda i: (i, 0, 0))],
        out_shape=[jax.ShapeDtypeStruct((t, d), F32), jax.ShapeDtypeStruct((t, d // LANES, LANES), F32),
                   jax.ShapeDtypeStruct((t, ROUTE_W), F32), jax.ShapeDtypeStruct((t // tm, 8, ROUTE_W), F32)],
        compiler_params=_cparams(("parallel",)),
    )(x, h, *[o for pair in zip(outs_ctx, outs_dec) for o in pair], mod, w_branch_b, w_bgate_b, w_out_b,
      norm2_g.reshape(1, d), wr, br)


MOE_BLOCK = 256
ISSUE_UNROLL = 8
assert TOP_K == 4


def _moe_kernel(be_ref, slot_ref, nact_ref, h_hbm, gate_ref, wg_ref, bg_ref, wd_ref, bd_ref, y_hbm,
                xbuf, ybuf, wg_b, wd_b, sem_in, sem_out):
    i = pl.program_id(0)
    n_act = nact_ref[0]
    blk = MOE_BLOCK
    n_tok = h_hbm.shape[0]
    cur = i % 2

    def start_gather(block, buf_slot):
        def issue(r, carry):
            tok = jnp.minimum(lax.shift_right_logical(slot_ref[block * blk + r], 2), n_tok - 1)
            pltpu.make_async_copy(h_hbm.at[tok], xbuf.at[buf_slot, r], sem_in.at[buf_slot]).start()
            return carry
        lax.fori_loop(0, blk, issue, 0, unroll=ISSUE_UNROLL)

    def wait_gather(buf_slot):
        pltpu.make_async_copy(h_hbm.at[pl.ds(0, blk)], xbuf.at[buf_slot], sem_in.at[buf_slot]).wait()

    def start_scatter(block, buf_slot):
        def issue(r2, carry):
            for prio in range(2):
                r = 2 * r2 + prio
                pltpu.make_async_copy(ybuf.at[buf_slot, r], y_hbm.at[slot_ref[block * blk + r]],
                                      sem_out.at[buf_slot]).start(priority=prio)
            return carry
        lax.fori_loop(0, blk // 2, issue, 0, unroll=ISSUE_UNROLL // 2)

    def wait_scatter(buf_slot):
        pltpu.make_async_copy(ybuf.at[buf_slot], y_hbm.at[pl.ds(0, blk)], sem_out.at[buf_slot]).wait()

    @pl.when(i == 0)
    def _():
        n_slots = y_hbm.shape[0] - 2 * blk
        ybuf[0] = jnp.zeros(ybuf.shape[1:], F32)
        for bank in range(2):
            fill = pltpu.make_async_copy(ybuf.at[0], y_hbm.at[pl.ds(n_slots + bank * blk, blk)], sem_out.at[0])
            fill.start()
            fill.wait()

    @pl.when((i == 0) & (n_act > 0))
    def _():
        start_gather(0, 0)

    @pl.when(i < n_act)
    def _():
        @pl.when(i + 1 < n_act)
        def _():
            start_gather(i + 1, 1 - cur)

        @pl.when((i == 0) | (be_ref[i] != be_ref[jnp.maximum(i - 1, 0)]))
        def _():
            rows = 128

            def cast(c, carry):
                sl = pl.ds(pl.multiple_of(c * rows, rows), rows)
                wg_b[sl, :] = wg_ref[0, sl, :].astype(BF16)
                wd_b[sl, :] = wd_ref[0, sl, :].astype(BF16)
                return carry

            lax.fori_loop(0, wg_b.shape[0] // rows, cast, 0)

        wait_gather(cur)
        x = _from_tiles(xbuf.at[cur]).astype(BF16)
        gu = _dot(x, wg_b[...]) + bg_ref[0]
        glu = jnp.minimum(gu[:, :D_FF], SWIGLU_LIMIT)
        lin = jnp.clip(gu[:, D_FF:], -SWIGLU_LIMIT, SWIGLU_LIMIT)
        act = (lin + 1.0) * glu * _sigmoid(SWIGLU_ALPHA * glu)
        y = (_dot(act.astype(BF16), wd_b[...]) + bd_ref[0]) * gate_ref[...]

        @pl.when(i >= 2)
        def _():
            wait_scatter(cur)

        _to_tiles(ybuf.at[cur], y)
        start_scatter(i, cur)

    @pl.when(i == pl.num_programs(0) - 1)
    def _():
        @pl.when(n_act >= 1)
        def _():
            wait_scatter((n_act - 1) % 2)

        @pl.when(n_act >= 2)
        def _():
            wait_scatter(n_act % 2)


def moe_experts(h3, block_e, row_slot, n_act, row_gate, w_gu, b_gu, w_dn, b_dn, n_slots):
    n_rows = row_slot.shape[0]
    n_blocks = n_rows // MOE_BLOCK
    e, d, f2 = w_gu.shape
    s = d // LANES
    grid_spec = pltpu.PrefetchScalarGridSpec(
        num_scalar_prefetch=3,
        grid=(n_blocks,),
        in_specs=[pl.BlockSpec(memory_space=pl.ANY),
                  pl.BlockSpec((MOE_BLOCK, 1), lambda i, be, sl, na: (i, 0)),
                  pl.BlockSpec((1, d, f2), lambda i, be, sl, na: (be[i], 0, 0)),
                  pl.BlockSpec((1, 1, f2), lambda i, be, sl, na: (be[i], 0, 0)),
                  pl.BlockSpec((1, f2 // 2, d), lambda i, be, sl, na: (be[i], 0, 0)),
                  pl.BlockSpec((1, 1, d), lambda i, be, sl, na: (be[i], 0, 0))],
        out_specs=pl.BlockSpec(memory_space=pl.ANY),
        scratch_shapes=[pltpu.VMEM((2, MOE_BLOCK, s, LANES), F32), pltpu.VMEM((2, MOE_BLOCK, s, LANES), F32),
                        pltpu.VMEM((d, f2), BF16), pltpu.VMEM((f2 // 2, d), BF16),
                        pltpu.SemaphoreType.DMA((2,)), pltpu.SemaphoreType.DMA((2,))],
    )
    return pl.pallas_call(
        _moe_kernel,
        grid_spec=grid_spec,
        out_shape=jax.ShapeDtypeStruct((n_slots + 2 * MOE_BLOCK, s, LANES), F32),
        compiler_params=_cparams(("arbitrary",)),
    )(block_e, row_slot, n_act, h3, row_gate.reshape(n_rows, 1), w_gu, b_gu.reshape(e, 1, f2),
      w_dn, b_dn.reshape(e, 1, d))


def _combine_kernel(y_ref, x1_ref, mod_ref, fg_ref, o_ref, *, final):
    tc = x1_ref.shape[0]
    y = jnp.concatenate(
        [sum(y_ref[pl.ds(j, tc, stride=TOP_K), s, :] for j in range(TOP_K)) for s in range(y_ref.shape[1])],
        axis=1)
    x2 = x1_ref[...] + mod_ref[0][5:6] * y
    if final:
        x2 = _rms(x2, fg_ref[...])
    o_ref[...] = x2


def moe_combine(ys3, x1, mod, mod_index, final_g, tc, final):
    t, d = x1.shape
    s = d // LANES
    return pl.pallas_call(
        functools.partial(_combine_kernel, final=final),
        grid=(t // tc,),
        in_specs=[pl.BlockSpec((tc * TOP_K, s, LANES), lambda i: (i, 0, 0)),
                  pl.BlockSpec((tc, d), lambda i: (i, 0)),
                  pl.BlockSpec((1, 8, d), lambda i: (mod_index(i), 0, 0)),
                  pl.BlockSpec((1, d), lambda i: (0, 0))],
        out_specs=pl.BlockSpec((tc, d), lambda i: (i, 0)),
        out_shape=jax.ShapeDtypeStruct((t, d), F32),
        compiler_params=_cparams(("parallel",)),
    )(ys3, x1, mod, final_g.reshape(1, d))


def _route_metadata(route, tile_counts, n_tok):
    gates = route[:, 0:TOP_K].reshape(-1)
    experts = route[:, TOP_K:2 * TOP_K].astype(jnp.int32).reshape(-1)
    n_slots = n_tok * TOP_K
    n_rows = n_slots + N_EXPERTS * MOE_BLOCK
    n_blocks = n_rows // MOE_BLOCK
    order = jnp.argsort(experts, stable=True).astype(jnp.int32)
    counts = jnp.sum(tile_counts[:, 0, :N_EXPERTS], axis=0).astype(jnp.int32)
    sort_start = jnp.cumsum(counts) - counts
    padded = ((counts + MOE_BLOCK - 1) // MOE_BLOCK) * MOE_BLOCK
    pad_end = jnp.cumsum(padded)
    pad_start = pad_end - padded
    block_start = jnp.arange(n_blocks, dtype=jnp.int32) * MOE_BLOCK
    block_e = jnp.minimum(jnp.sum((pad_end[None, :] <= block_start[:, None]).astype(jnp.int32), axis=1),
                          N_EXPERTS - 1)
    row = jnp.arange(n_rows, dtype=jnp.int32)
    per_row = lambda table: jnp.repeat(table[block_e], MOE_BLOCK)
    rank = row - per_row(pad_start)
    valid = rank < per_row(counts)
    src = order[jnp.clip(per_row(sort_start) + rank, 0, n_slots - 1)]
    scratch_row = n_slots + ((row // MOE_BLOCK) % 2) * MOE_BLOCK + row % MOE_BLOCK
    row_slot = jnp.where(valid, src, scratch_row).astype(jnp.int32)
    row_gate = jnp.where(valid, gates[src], 0.0)
    n_act = (pad_end[-1] // MOE_BLOCK).astype(jnp.int32).reshape(1)
    return block_e, row_slot, n_act, row_gate


def _split3(x):
    hi = x.astype(BF16)
    r1 = x - hi.astype(F32)
    mid = r1.astype(BF16)
    lo = (r1 - mid.astype(F32)).astype(BF16)
    return hi, mid, lo


def _dot_exact_rhs(x, c_b):
    hi, mid, lo = _split3(x)
    return _dot(hi, c_b) + _dot(mid, c_b) + _dot(lo, c_b)


def _dot_exact_lhs(c_b, x):
    hi, mid, lo = _split3(x)
    return _dot(c_b, hi) + _dot(c_b, mid) + _dot(c_b, lo)


def _group_matrix(n, group, value):
    r = lax.broadcasted_iota(jnp.int32, (n, n), 0) // group
    c = lax.broadcasted_iota(jnp.int32, (n, n), 1) // group
    return jnp.where(r == c, value, 0.0).astype(BF16)


def _head_rms(x, gmat, g):
    ms = _dot_exact_rhs(x * x, gmat)
    return x * lax.rsqrt(ms + EPS) * g


def _conv3(u, w, first, last):
    n = u.shape[0]
    prev = jnp.where(first, 0.0, pltpu.roll(u, 1, 0))
    nxt = jnp.where(last, 0.0, pltpu.roll(u, n - 1, 0))
    return prev * w[0:1] + u * w[1:2] + nxt * w[2:3]


def _rope_tables(n_tok):
    half = DA_DK // 4
    freqs = (ROPE_BASE ** (-np.arange(half, dtype=np.float32) / half)).astype(np.float32)
    t = np.arange(n_tok)
    row = (t // GRID_W).astype(np.float32)
    col = (t % GRID_W).astype(np.float32)
    cos32 = np.zeros((n_tok, DA_DK), np.float64)
    sin32 = np.zeros((n_tok, DA_DK), np.float64)
    for base, pos in ((0, row), (2 * half, col)):
        ang = (pos[:, None] * freqs[None, :]).astype(np.float32).astype(np.float64)
        cos32[:, base:base + half] = np.cos(ang)
        cos32[:, base + half:base + 2 * half] = np.cos(ang)
        sin32[:, base:base + half] = -np.sin(ang)
        sin32[:, base + half:base + 2 * half] = np.sin(ang)
    reps = DA_H * 2
    return (jnp.asarray(np.tile(cos32, (1, reps)), F32), jnp.asarray(np.tile(sin32, (1, reps)), F32))


def _rope(x, cos, sin):
    n = x.shape[1]
    lane = lax.broadcasted_iota(jnp.int32, x.shape, 1)
    partner = jnp.where((lane % 16) < 8, pltpu.roll(x, n - 8, 1), pltpu.roll(x, 8, 1))
    return x * cos + partner * sin


def _attn_core(q, kall, vall, lam, g, o_ref, lam_init):
    tq = q.shape[0]
    lane = lax.broadcasted_iota(jnp.int32, (1, 256), 1)
    acc = jnp.zeros((tq, 256), F32)
    for h in range(DA_H):
        probs = []
        for m in range(2):
            c0 = h * 2 * DA_DK + m * DA_DK
            qm = jnp.where((lane >= c0) & (lane < c0 + DA_DK), q, 0.0).astype(BF16)
            s = _dot_nt(qm, kall)
            s = s - jnp.max(s, axis=-1, keepdims=True)
            p = jnp.exp(s)
            probs.append(p / jnp.sum(p, axis=-1, keepdims=True))
        a = (probs[0] - lam * probs[1]).astype(BF16)
        vh = jnp.where((lane >= h * DA_DV) & (lane < (h + 1) * DA_DV), vall, jnp.zeros_like(vall))
        acc = acc + _dot(a, vh)
    gmat = _group_matrix(256, DA_DV, 1.0 / DA_DV)
    o_ref[...] = (_head_rms(acc, gmat, g) * (1.0 - lam_init)).astype(o_ref.dtype)


def _lam(lp):
    return (jnp.exp(jnp.sum(lp[0:1] * lp[1:2], axis=-1, keepdims=True))
            - jnp.exp(jnp.sum(lp[2:3] * lp[3:4], axis=-1, keepdims=True)))


def _attn_ctx_kernel(lp_ref, q_ref, k_ref, v_ref, g_ref, o_ref, *, lam_init):
    lam = _lam(lp_ref[...]) + lam_init
    q = q_ref[...] * (DA_DK ** -0.5)
    _attn_core(q, k_ref[...].astype(BF16), v_ref[...].astype(BF16), lam, g_ref[...], o_ref, lam_init)


def _attn_dec_kernel(lp_ref, q_ref, k_ref, v_ref, ck_ref, cv_ref, cosq_ref, sinq_ref, cosk_ref, sink_ref,
                     g_ref, o_ref, kall, vall, *, lam_init):
    n_lat = k_ref.shape[0]
    n_past = ck_ref.shape[1]

    @pl.when(pl.program_id(1) == 0)
    def _():
        kall[0:n_lat, :] = _rope(k_ref[...], cosk_ref[...], sink_ref[...]).astype(BF16)
        kall[n_lat:n_lat + n_past, :] = ck_ref[0].astype(BF16)
        vall[0:n_lat, :] = v_ref[...].astype(BF16)
        vall[n_lat:n_lat + n_past, :] = cv_ref[0].astype(BF16)

    lam = _lam(lp_ref[...]) + lam_init
    q = _rope(q_ref[...], cosq_ref[...], sinq_ref[...]) * (DA_DK ** -0.5)
    _attn_core(q, kall[...], vall[...], lam, g_ref[...], o_ref, lam_init)


def attention_ctx(u, lam_p, norm_g, layer, n_b, seq, row0):
    lam_init = 0.8 - 0.6 * math.exp(-0.3 * layer)
    rb = row0 // seq
    cq, ck, cv = U_DA // 256, U_DA // 256 + 1, U_DA // 256 + 2
    return pl.pallas_call(
        functools.partial(_attn_ctx_kernel, lam_init=lam_init),
        grid=(n_b,),
        in_specs=[pl.BlockSpec((4, DA_DK), lambda b: (0, 0)),
                  pl.BlockSpec((seq, 256), lambda b: (rb + b, cq)),
                  pl.BlockSpec((seq, 256), lambda b: (rb + b, ck)),
                  pl.BlockSpec((seq, 256), lambda b: (rb + b, cv)),
                  pl.BlockSpec((1, 256), lambda b: (0, 0))],
        out_specs=pl.BlockSpec((seq, 256), lambda b: (b, 0)),
        out_shape=jax.ShapeDtypeStruct((n_b * seq, 256), BF16),
        compiler_params=_cparams(("parallel",)),
    )(lam_p, u, u, u, norm_g.reshape(1, 256))


def attention_dec(u, cache_k, cache_v, lam_p, norm_g, layer, n_b, seq, row0, tq=256):
    lam_init = 0.8 - 0.6 * math.exp(-0.3 * layer)
    rb = row0 // seq
    rq = row0 // tq
    nq = seq // tq
    n_past = cache_k.shape[1]
    cq, ck, cv = U_DA // 256, U_DA // 256 + 1, U_DA // 256 + 2
    cos, sin = _rope_tables(seq)
    return pl.pallas_call(
        functools.partial(_attn_dec_kernel, lam_init=lam_init),
        grid=(n_b, nq),
        in_specs=[pl.BlockSpec((4, DA_DK), lambda b, i: (0, 0)),
                  pl.BlockSpec((tq, 256), lambda b, i: (rq + b * nq + i, cq)),
                  pl.BlockSpec((seq, 256), lambda b, i: (rb + b, ck)),
                  pl.BlockSpec((seq, 256), lambda b, i: (rb + b, cv)),
                  pl.BlockSpec((1, n_past, 256), lambda b, i: (b, 0, 0)),
                  pl.BlockSpec((1, n_past, 256), lambda b, i: (b, 0, 0)),
                  pl.BlockSpec((tq, 256), lambda b, i: (i, 0)),
                  pl.BlockSpec((tq, 256), lambda b, i: (i, 0)),
                  pl.BlockSpec((seq, 256), lambda b, i: (0, 0)),
                  pl.BlockSpec((seq, 256), lambda b, i: (0, 0)),
                  pl.BlockSpec((1, 256), lambda b, i: (0, 0))],
        out_specs=pl.BlockSpec((tq, 256), lambda b, i: (b * nq + i, 0)),
        out_shape=jax.ShapeDtypeStruct((n_b * seq, 256), BF16),
        scratch_shapes=[pltpu.VMEM((seq + n_past, 256), BF16), pltpu.VMEM((seq + n_past, 256), BF16)],
        compiler_params=_cparams(("parallel", "arbitrary")),
    )(lam_p, u, u, u, cache_k, cache_v, cos, sin, cos, sin, norm_g.reshape(1, 256))


def _conv_chunk(u_ref, cw, c, n_chunks, col0, ncol, base=0):
    end = base + n_chunks * CHUNK
    r0 = pl.multiple_of(base + c * CHUNK, CHUNK)
    x = u_ref[pl.ds(r0, CHUNK), col0:col0 + ncol]
    hp = u_ref[pl.ds(pl.multiple_of(jnp.maximum(r0 - 8, base), 8), 8), col0:col0 + ncol][7:8]
    hn = u_ref[pl.ds(pl.multiple_of(jnp.minimum(r0 + CHUNK, end - 8), 8), 8), col0:col0 + ncol][0:1]
    hp = jnp.where(c == 0, 0.0, hp)
    hn = jnp.where(c == n_chunks - 1, 0.0, hn)
    row = lax.broadcasted_iota(jnp.int32, (CHUNK, 1), 0)
    prev = jnp.where(row == 0, hp, pltpu.roll(x, 1, 0))
    nxt = jnp.where(row == CHUNK - 1, hn, pltpu.roll(x, CHUNK - 1, 0))
    return prev * cw[0:1] + x * cw[1:2] + nxt * cw[2:3]


def _cumsum_rows(x):
    row = lax.broadcasted_iota(jnp.int32, (x.shape[0], 1), 0)
    s = 1
    while s < x.shape[0]:
        x = x + jnp.where(row >= s, pltpu.roll(x, s, 0), 0.0)
        s *= 2
    return x


def _time_cumsum(x, reverse):
    p = _cumsum_rows(x)
    total = p[x.shape[0] - 1:x.shape[0]]
    if reverse:
        p = total - p + x
    return p, total


def _before_masks(reverse):
    t = lax.broadcasted_iota(jnp.int32, (CHUNK, CHUNK), 0)
    s = lax.broadcasted_iota(jnp.int32, (CHUNK, CHUNK), 1)
    if reverse:
        return s > t, s >= t
    return s < t, s <= t


def _mm(a, b):
    return _dot(a.astype(BF16), b.astype(BF16))


def _mm_nt(a, b):
    return _dot_nt(a.astype(BF16), b.astype(BF16))


def _mm_tn(a, b):
    return _dot_tn(a.astype(BF16), b.astype(BF16))


def _rwkv_chunk(uc, d, prm, s_ref):
    w0, w2, a0, a2, k_k, k_a, r_k = prm
    reverse = d == 1
    r = uc[:, 0:256]
    kb = uc[:, 256:512]
    v = uc[:, 512:768]
    wd = uc[:, 768:832]
    ad = uc[:, 832:896]
    ones_g = _group_matrix(RW_W, RW_N, 1.0)
    kk = kb * k_k
    kk = kk * lax.rsqrt(_dot_exact_rhs(kk * kk, ones_g) + EPS)
    w_pre = w0[d:d + 1] + _dot(jnp.tanh(wd), w2[d], HI)
    softplus_neg = jnp.maximum(-w_pre, 0.0) + jnp.log(1.0 + jnp.exp(-jnp.abs(w_pre)))
    logw = -jnp.exp(-softplus_neg - 0.5)
    a = _sigmoid(a0[d:d + 1] + _dot(ad, a2[d], HI))
    k = kb * (1.0 + (a - 1.0) * k_a)
    b = kk * a
    cum, total = _time_cumsum(logw, reverse)
    e_in = jnp.exp(cum)
    e_out = jnp.exp(-cum)
    kap = kk * jnp.exp(cum - logw)
    bet = b * e_out
    kt = k * e_out
    rt = r * e_in
    g_end = jnp.exp(total)
    n = RW_H * CHUNK
    row = lax.broadcasted_iota(jnp.int32, (n, n), 0)
    col = lax.broadcasted_iota(jnp.int32, (n, n), 1)
    same_head = (row // CHUNK) == (col // RW_N)
    t_row, t_col = row % CHUNK, col % CHUNK
    strict = (t_col > t_row) if reverse else (t_col < t_row)
    incl = (t_col >= t_row) if reverse else (t_col <= t_row)
    eye = (row == col).astype(F32)

    def bd(x):
        xb = x.astype(BF16)
        return jnp.where(same_head, jnp.concatenate([xb] * RW_H, axis=0), jnp.zeros((n, n), BF16))

    kap_b, rt_b, bet_b, kt_b, v_b = bd(kap), bd(rt), bd(bet), bd(kt), bd(v)
    x = jnp.concatenate([kap_b, rt_b], axis=0)
    gb = _dot_nt(x, bet_b)
    gk = _dot_nt(x, kt_b)
    a_b = jnp.where(strict, gb[:n], 0.0)
    m_b = jnp.where(incl, gb[n:], 0.0)
    a_k = jnp.where(strict, gk[:n], 0.0)
    m_k = jnp.where(incl, gk[n:], 0.0)
    npow = -a_b
    tinv = eye + npow
    for _ in range(5):
        npow = _mm(npow, npow)
        tinv = tinv + _mm(tinv, npow)
    s0 = s_ref[...]
    xs = _dot_nt(x, s0.astype(BF16))
    u = -_mm(tinv, xs[:n] + _dot(a_k.astype(BF16), v_b))
    uv = jnp.concatenate([u.astype(BF16), v_b], axis=0)
    y_bd = xs[n:] + _dot(jnp.concatenate([m_b, m_k], axis=1).astype(BF16), uv)
    s_ref[...] = (s0 + _dot_tn(uv, jnp.concatenate([bet_b, kt_b], axis=0))) * g_end
    y = y_bd[0:CHUNK]
    for h in range(1, RW_H):
        y = y + y_bd[h * CHUNK:(h + 1) * CHUNK]
    bonus = _dot_exact_rhs(r * k * r_k, ones_g) * v
    return y, bonus


def _rwkv_kernel(*refs, has_state):
    if has_state:
        (u_ref, cw_ref, w0_ref, w2_ref, a0_ref, a2_ref, g2_ref, vec_ref, s0_ref, o_ref, sf_ref) = refs[:11]
        scratch = refs[11:]
    else:
        (u_ref, cw_ref, w0_ref, w2_ref, a0_ref, a2_ref, g2_ref, vec_ref, o_ref, sf_ref) = refs[:10]
        scratch = refs[10:]
    s_refs, (y_refs, bonus_refs) = scratch[:-4], (scratch[-4:-2], scratch[-2:])
    n_seq = sf_ref.shape[0]
    seq = u_ref.shape[0] // n_seq
    n_chunks = seq // CHUNK
    for g in range(n_seq):
        for d in range(2):
            s_ref = s_refs[2 * g + d]
            s_ref[...] = jnp.zeros(s_ref.shape, F32)
            if has_state:
                for h in range(RW_H):
                    s_ref[h * RW_N:(h + 1) * RW_N, h * RW_N:(h + 1) * RW_N] = s0_ref[g, d, h]
    vec = vec_ref[...]
    prm = (w0_ref[...], w2_ref, a0_ref[...], a2_ref, vec[0:1], vec[1:2], vec[2:3])
    cw = cw_ref[...]

    def step(j, carry):
        for g in range(n_seq):
            for d in range(2):
                c = j if d == 0 else n_chunks - 1 - j
                uc = _conv_chunk(u_ref, cw, c, n_chunks, 0, 1024, base=g * seq)
                y, bonus = _rwkv_chunk(uc, d, prm, s_refs[2 * g + d])
                rows = pl.ds(pl.multiple_of(g * seq + c * CHUNK, CHUNK), CHUNK)
                y_refs[d][rows, :] = y
                bonus_refs[d][rows, :] = bonus
        return carry

    lax.fori_loop(0, n_chunks, step, 0)
    for g in range(n_seq):
        for d in range(2):
            for h in range(RW_H):
                sf_ref[g, d, h] = s_refs[2 * g + d][h * RW_N:(h + 1) * RW_N, h * RW_N:(h + 1) * RW_N]

    mean_g = _group_matrix(RW_W, RW_N, 1.0 / RW_N)

    def finish(c, carry):
        for g in range(n_seq):
            rows = pl.ds(pl.multiple_of(g * seq + c * CHUNK, CHUNK), CHUNK)
            gd = _conv_chunk(u_ref, cw[:, 896:1024], c, n_chunks, 896, 128, base=g * seq)
            gate = _mm(_sigmoid(gd), g2_ref[...])
            y = y_refs[0][rows, :] + y_refs[1][rows, :]
            bonus = bonus_refs[0][rows, :] + bonus_refs[1][rows, :]
            yc = y - _dot_exact_rhs(y, mean_g)
            o_ref[rows, :] = ((_head_rms(yc, mean_g, vec[3:4]) + bonus) * gate).astype(o_ref.dtype)
        return carry

    lax.fori_loop(0, n_chunks, finish, 0)


def _transpose_small(x):
    eye = (lax.broadcasted_iota(jnp.int32, (CHUNK, CHUNK), 0)
           == lax.broadcasted_iota(jnp.int32, (CHUNK, CHUNK), 1)).astype(BF16)
    hi, mid, lo = _split3(x)
    return _dot_tn(hi, eye) + _dot_tn(mid, eye) + _dot_tn(lo, eye)


def _mlstm_chunk(qk, v, small, d, bias, c_ref, n_ref, m_ref):
    reverse = d == 1
    lane = lax.broadcasted_iota(jnp.int32, (1, LANES), 1)
    pre = small + bias
    lf = jnp.where((lane >= 8) & (lane < 16), _log_sigmoid(pre), 0.0)
    fcum, ftot = _time_cumsum(lf, reverse)
    w = jnp.where((lane >= 8) & (lane < 16), fcum, pre)
    wt = _transpose_small(w)
    n = ML_H * CHUNK
    heads = [d * ML_H + h for h in range(ML_H)]
    rows_of = lambda pieces: jnp.concatenate(pieces, axis=0)
    cols_of = lambda pieces: jnp.concatenate(pieces, axis=1)
    f_c = rows_of([w[:, 8 + j:9 + j] for j in heads])
    i_c = rows_of([w[:, j:j + 1] for j in heads])
    f_r = cols_of([wt[8 + j:9 + j, :] for j in heads])
    i_r = cols_of([wt[j:j + 1, :] for j in heads])
    f_l = [ftot[:, 8 + j:9 + j] for j in heads]
    m_prev = [m_ref[j:j + 1, 0:1] for j in heads]
    col1 = lambda vals: rows_of([jnp.broadcast_to(x, (CHUNK, 1)) for x in vals])
    row = lax.broadcasted_iota(jnp.int32, (n, n), 0)
    col = lax.broadcasted_iota(jnp.int32, (n, n), 1)
    t_row, t_col = row % CHUNK, col % CHUNK
    incl = ((row // CHUNK) == (col // CHUNK)) & ((t_col >= t_row) if reverse else (t_col <= t_row))
    row_k = lax.broadcasted_iota(jnp.int32, (n, ML_H * ML_DK), 0)
    col_k = lax.broadcasted_iota(jnp.int32, (n, ML_H * ML_DK), 1)
    head_k = (row_k // CHUNK) == (col_k // ML_DK)
    head_v = (row // CHUNK) == (col // ML_DV)
    q_bd = jnp.where(head_k, jnp.concatenate([qk[:, 0:128]] * ML_H, axis=0), 0.0)
    k_bd = jnp.where(head_k, jnp.concatenate([qk[:, 128:256] * (ML_DK ** -0.5)] * ML_H, axis=0), 0.0)
    v_bd = jnp.where(head_v, jnp.concatenate([v] * ML_H, axis=0), 0.0)
    c_prev = c_ref[d]
    n_prev = cols_of([n_ref[j:j + 1, :] for j in heads])
    dm = jnp.where(incl, f_c - f_r + i_r, -jnp.inf)
    inter = f_c + col1(m_prev)
    mt = jnp.maximum(inter, jnp.max(dm, axis=-1, keepdims=True))
    wi = jnp.exp(dm - mt)
    we = jnp.exp(inter - mt)
    q_b = q_bd.astype(BF16)
    k_b = k_bd.astype(BF16)
    s = _dot_nt(q_b, k_b) * wi
    num = we * _dot_nt(q_b, c_prev.astype(BF16)) + _mm(s, v_bd)
    den = we * jnp.sum(q_bd * n_prev, axis=-1, keepdims=True) + jnp.sum(s, axis=-1, keepdims=True)
    h_bd = num / jnp.maximum(jnp.abs(den), jnp.exp(-mt))
    out = h_bd[0:CHUNK]
    for h in range(1, ML_H):
        out = out + h_bd[h * CHUNK:(h + 1) * CHUNK]
    gs_c = col1(f_l) - f_c + i_c
    m_new = [jnp.maximum(f_l[h] + m_prev[h],
                         jnp.max(gs_c[h * CHUNK:(h + 1) * CHUNK], axis=0, keepdims=True)) for h in range(ML_H)]
    ws = jnp.exp(gs_c - col1(m_new))
    wc = [jnp.exp(f_l[h] + m_prev[h] - m_new[h]) for h in range(ML_H)]
    wc_rows = rows_of([jnp.broadcast_to(x, (ML_DV, 1)) for x in wc])
    wc_cols = cols_of([jnp.broadcast_to(x, (1, ML_DK)) for x in wc])
    c_ref[d] = wc_rows * c_prev + _dot_tn((v_bd * ws).astype(BF16), k_b)
    n_new = wc_cols * n_prev + jnp.sum(k_bd * ws, axis=0, keepdims=True)
    for h, j in enumerate(heads):
        n_ref[j:j + 1, :] = n_new[:, h * ML_DK:(h + 1) * ML_DK]
        m_ref[j:j + 1, :] = jnp.broadcast_to(m_new[h], (1, LANES))
    return out


def _mlstm_kernel(*refs, has_state):
    if has_state:
        (qk_ref, v_ref, og_ref, sm_ref, cw_ref, bias_ref, g_ref, c0_ref, n0_ref, m0_ref,
         o_ref, cf_ref, nf_ref, mf_ref, c_ref, n_ref, m_ref, h_ref) = refs
        c_ref[...] = jnp.zeros(c_ref.shape, F32)
        for d in range(2):
            for h in range(ML_H):
                c_ref[d, h * ML_DV:(h + 1) * ML_DV, h * ML_DK:(h + 1) * ML_DK] = c0_ref[0, d, h]
        n_ref[...] = n0_ref[0]
        m_ref[...] = m0_ref[0]
    else:
        (qk_ref, v_ref, og_ref, sm_ref, cw_ref, bias_ref, g_ref,
         o_ref, cf_ref, nf_ref, mf_ref, c_ref, n_ref, m_ref, h_ref) = refs
        c_ref[...] = jnp.zeros(c_ref.shape, F32)
        n_ref[...] = jnp.zeros(n_ref.shape, F32)
        m_ref[...] = jnp.zeros(m_ref.shape, F32)
    n_chunks = qk_ref.shape[0] // CHUNK
    h_ref[...] = jnp.zeros(h_ref.shape, F32)
    cw = cw_ref[...]
    bias = bias_ref[...]

    def step(j, carry):
        for d in range(2):
            c = j if d == 0 else n_chunks - 1 - j
            rows = pl.ds(pl.multiple_of(c * CHUNK, CHUNK), CHUNK)
            qk = _conv_chunk(qk_ref, cw, c, n_chunks, 0, 256)
            h_ref[rows, :] += _mlstm_chunk(qk, v_ref[rows, :], sm_ref[rows, :], d, bias,
                                           c_ref, n_ref, m_ref)
        return carry

    lax.fori_loop(0, n_chunks, step, 0)
    for d in range(2):
        for h in range(ML_H):
            cf_ref[0, d, h] = c_ref[d, h * ML_DV:(h + 1) * ML_DV, h * ML_DK:(h + 1) * ML_DK]
    nf_ref[0] = n_ref[...]
    mf_ref[0] = m_ref[...]
    gmat = _group_matrix(ML_H * ML_DV, ML_DV, 1.0 / ML_DV)

    def finish(c, carry):
        rows = pl.ds(pl.multiple_of(c * CHUNK, CHUNK), CHUNK)
        o_ref[rows, :] = (_head_rms(h_ref[rows, :], gmat, g_ref[...])
                          * _sigmoid(og_ref[rows, :])).astype(o_ref.dtype)
        return carry

    lax.fori_loop(0, n_chunks, finish, 0)


def mlstm_branch(u, conv_w, p, layer, n_b, seq, row0, state0=None):
    i = layer
    rb = row0 // seq
    has_state = state0 is not None
    bias = jnp.zeros((1, LANES), F32)
    bias = bias.at[0, 0:8].set(p['ml_i_bias'][i].reshape(8)).at[0, 8:16].set(p['ml_f_bias'][i].reshape(8))
    full = lambda *shape: pl.BlockSpec(shape, lambda b: (0,) * len(shape))
    in_specs = [pl.BlockSpec((seq, 256), lambda b: (rb + b, U_MLQK // 256)),
                pl.BlockSpec((seq, 256), lambda b: (rb + b, U_MLVO // 256)),
                pl.BlockSpec((seq, 256), lambda b: (rb + b, U_MLVO // 256 + 1)),
                pl.BlockSpec((seq, LANES), lambda b: (rb + b, U_SMALL // LANES)),
                full(3, 256), full(1, LANES), full(1, 256)]
    args = [u, u, u, u, conv_w[:, U_MLQK:U_MLQK + 256], bias, p['ml_norm_g'][i].reshape(1, 256)]
    if has_state:
        c0, n0, m0 = state0
        in_specs += [pl.BlockSpec((1, 2, ML_H, ML_DV, ML_DK), lambda b: (b, 0, 0, 0, 0)),
                     pl.BlockSpec((1, 8, ML_DK), lambda b: (b, 0, 0)),
                     pl.BlockSpec((1, 8, LANES), lambda b: (b, 0, 0))]
        args += [c0, n0.reshape(n_b, 8, ML_DK),
                 jnp.broadcast_to(m0.reshape(n_b, 8, 1), (n_b, 8, LANES))]
    o, cf, nf, mf = pl.pallas_call(
        functools.partial(_mlstm_kernel, has_state=has_state),
        grid=(n_b,),
        in_specs=in_specs,
        out_specs=[pl.BlockSpec((seq, 256), lambda b: (b, 0)),
                   pl.BlockSpec((1, 2, ML_H, ML_DV, ML_DK), lambda b: (b, 0, 0, 0, 0)),
                   pl.BlockSpec((1, 8, ML_DK), lambda b: (b, 0, 0)),
                   pl.BlockSpec((1, 8, LANES), lambda b: (b, 0, 0))],
        out_shape=[jax.ShapeDtypeStruct((n_b * seq, 256), BF16),
                   jax.ShapeDtypeStruct((n_b, 2, ML_H, ML_DV, ML_DK), F32),
                   jax.ShapeDtypeStruct((n_b, 8, ML_DK), F32),
                   jax.ShapeDtypeStruct((n_b, 8, LANES), F32)],
        scratch_shapes=[pltpu.VMEM((2, ML_H * ML_DV, ML_H * ML_DK), F32), pltpu.VMEM((8, ML_DK), F32),
                        pltpu.VMEM((8, LANES), F32), pltpu.VMEM((seq, 256), F32)],
        compiler_params=_cparams(("parallel",)),
    )(*args)
    return o, cf, nf.reshape(n_b, 2, ML_H, ML_DK), mf[:, :, 0].reshape(n_b, 2, ML_H)


def _gla_chunk(qk, v, small, d, a2_ref, abias, st_ref):
    reverse = d == 1
    q = qk[:, 0:128] * (GL_DK ** -0.5)
    k = qk[:, 128:256]
    la = _log_sigmoid(_dot(small, a2_ref[d], HI) + abias[d:d + 1]) * (1.0 / GL_TAU)
    cum, total = _time_cumsum(la, reverse)
    mid = cum[CHUNK // 2:CHUNK // 2 + 1]
    qe = q * jnp.exp(cum - mid)
    ke = k * jnp.exp(mid - cum)
    qg = q * jnp.exp(cum)
    kg = k * jnp.exp(total - cum)
    g_end = jnp.exp(total)
    n = GL_H * CHUNK
    row = lax.broadcasted_iota(jnp.int32, (n, n), 0)
    col = lax.broadcasted_iota(jnp.int32, (n, n), 1)
    t_row, t_col = row % CHUNK, col % CHUNK
    incl = ((row // CHUNK) == (col // CHUNK)) & ((t_col >= t_row) if reverse else (t_col <= t_row))
    row_k = lax.broadcasted_iota(jnp.int32, (n, GL_H * GL_DK), 0)
    col_k = lax.broadcasted_iota(jnp.int32, (n, GL_H * GL_DK), 1)
    head_k = (row_k // CHUNK) == (col_k // GL_DK)
    head_v = (row // CHUNK) == (col // GL_DV)

    def bd(x, mask):
        xb = x.astype(BF16)
        return jnp.where(mask, jnp.concatenate([xb] * GL_H, axis=0), jnp.zeros(mask.shape, BF16))

    v_bd = bd(v, head_v)
    a = jnp.where(incl, _dot_nt(bd(qe, head_k), bd(ke, head_k)), 0.0)
    st = st_ref[d]
    o_bd = _dot_nt(bd(qg, head_k), st.astype(BF16)) + _dot(a.astype(BF16), v_bd)
    st_ref[d] = st * g_end + _dot_tn(v_bd, bd(kg, head_k))
    out = o_bd[0:CHUNK]
    for h in range(1, GL_H):
        out = out + o_bd[h * CHUNK:(h + 1) * CHUNK]
    return out


def _gla_kernel(*refs, has_state):
    if has_state:
        (qk_ref, v_ref, gg_ref, sm_ref, cw_ref, a2_ref, ab_ref, g_ref, s0_ref,
         o_ref, sf_ref, st_ref, acc_ref) = refs
        st_ref[...] = jnp.zeros(st_ref.shape, F32)
        for d in range(2):
            for h in range(GL_H):
                st_ref[d, h * GL_DV:(h + 1) * GL_DV, h * GL_DK:(h + 1) * GL_DK] = s0_ref[0, d, h]
    else:
        (qk_ref, v_ref, gg_ref, sm_ref, cw_ref, a2_ref, ab_ref, g_ref,
         o_ref, sf_ref, st_ref, acc_ref) = refs
        st_ref[...] = jnp.zeros(st_ref.shape, F32)
    n_chunks = qk_ref.shape[0] // CHUNK
    acc_ref[...] = jnp.zeros(acc_ref.shape, F32)
    cw = cw_ref[...]
    abias = ab_ref[...]

    def step(j, carry):
        for d in range(2):
            c = j if d == 0 else n_chunks - 1 - j
            rows = pl.ds(pl.multiple_of(c * CHUNK, CHUNK), CHUNK)
            qk = _conv_chunk(qk_ref, cw, c, n_chunks, 0, 256)
            acc_ref[rows, :] += _gla_chunk(qk, v_ref[rows, :], sm_ref[rows, :], d, a2_ref, abias, st_ref)
        return carry

    lax.fori_loop(0, n_chunks, step, 0)
    for d in range(2):
        for h in range(GL_H):
            sf_ref[0, d, h] = st_ref[d, h * GL_DV:(h + 1) * GL_DV, h * GL_DK:(h + 1) * GL_DK]
    gmat = _group_matrix(GL_H * GL_DV, GL_DV, 1.0 / GL_DV)

    def finish(c, carry):
        rows = pl.ds(pl.multiple_of(c * CHUNK, CHUNK), CHUNK)
        gg = gg_ref[rows, :]
        o_ref[rows, :] = (_head_rms(acc_ref[rows, :], gmat, g_ref[...]) * (gg * _sigmoid(gg))).astype(o_ref.dtype)
        return carry

    lax.fori_loop(0, n_chunks, finish, 0)


def gla_branch(u, conv_w, p, layer, n_b, seq, row0, state0=None):
    i = layer
    rb = row0 // seq
    has_state = state0 is not None
    a2 = jnp.zeros((2, LANES, GL_H * GL_DK), F32).at[:, 16:32, :].set(p['gl_a2'][i])
    full = lambda *shape: pl.BlockSpec(shape, lambda b: (0,) * len(shape))
    in_specs = [pl.BlockSpec((seq, 256), lambda b: (rb + b, U_GLQK // 256)),
                pl.BlockSpec((seq, 256), lambda b: (rb + b, U_GLVG // 256)),
                pl.BlockSpec((seq, 256), lambda b: (rb + b, U_GLVG // 256 + 1)),
                pl.BlockSpec((seq, LANES), lambda b: (rb + b, U_SMALL // LANES)),
                full(3, 256), full(2, LANES, GL_H * GL_DK), full(2, GL_H * GL_DK), full(1, 256)]
    args = [u, u, u, u, conv_w[:, U_GLQK:U_GLQK + 256], a2, p['gl_a_bias'][i],
            p['gl_norm_g'][i].reshape(1, 256)]
    if has_state:
        in_specs.append(pl.BlockSpec((1, 2, GL_H, GL_DV, GL_DK), lambda b: (b, 0, 0, 0, 0)))
        args.append(jnp.swapaxes(state0, -1, -2))
    o, sf = pl.pallas_call(
        functools.partial(_gla_kernel, has_state=has_state),
        grid=(n_b,),
        in_specs=in_specs,
        out_specs=[pl.BlockSpec((seq, 256), lambda b: (b, 0)),
                   pl.BlockSpec((1, 2, GL_H, GL_DV, GL_DK), lambda b: (b, 0, 0, 0, 0))],
        out_shape=[jax.ShapeDtypeStruct((n_b * seq, 256), BF16),
                   jax.ShapeDtypeStruct((n_b, 2, GL_H, GL_DV, GL_DK), F32)],
        scratch_shapes=[pltpu.VMEM((2, GL_H * GL_DV, GL_H * GL_DK), F32), pltpu.VMEM((seq, 256), F32)],
        compiler_params=_cparams(("parallel",)),
    )(*args)
    return o, jnp.swapaxes(sf, -1, -2)


RW_SEQS_PER_STEP = 2


def rwkv_branch(u, conv_w, p, layer, n_b, seq, row0, state0=None):
    i = layer
    g = RW_SEQS_PER_STEP
    rows = g * seq
    rb = row0 // rows
    vec = jnp.stack([p['rw_k_k'][i], p['rw_k_a'][i], p['rw_r_k'][i], p['rw_norm_g'][i]])
    has_state = state0 is not None
    full = lambda *shape: pl.BlockSpec(shape, lambda b: (0,) * len(shape))
    in_specs = [pl.BlockSpec((rows, 1024), lambda b: (rb + b, 0), pipeline_mode=pl.Buffered(1)),
                full(3, 1024), full(2, RW_W), full(2, 64, RW_W), full(2, RW_W), full(2, 64, RW_W),
                full(128, RW_W), full(4, RW_W)]
    args = [u, conv_w[:, 0:1024], p['rw_w0'][i], p['rw_w2'][i], p['rw_a0'][i], p['rw_a2'][i],
            p['rw_g2'][i], vec]
    if has_state:
        in_specs.append(pl.BlockSpec((g, 2, RW_H, RW_N, RW_N), lambda b: (b, 0, 0, 0, 0)))
        args.append(state0)
    return pl.pallas_call(
        functools.partial(_rwkv_kernel, has_state=has_state),
        grid=(n_b // g,),
        in_specs=in_specs,
        out_specs=[pl.BlockSpec((rows, RW_W), lambda b: (b, 0)),
                   pl.BlockSpec((g, 2, RW_H, RW_N, RW_N), lambda b: (b, 0, 0, 0, 0))],
        out_shape=[jax.ShapeDtypeStruct((n_b * seq, RW_W), BF16),
                   jax.ShapeDtypeStruct((n_b, 2, RW_H, RW_N, RW_N), F32)],
        scratch_shapes=([pltpu.VMEM((RW_W, RW_W), F32)] * (2 * g) + [pltpu.VMEM((rows, RW_W), F32)] * 4),
        compiler_params=_cparams(("parallel",)),
    )(*args)


TOKEN_TILE = 512


def kernel(x_prompt, x_sample, cache_attn_k, cache_attn_v, state_rwkv, state_mlstm_c, state_mlstm_n,
           state_mlstm_m, state_gla, c, c_ctx, norm1_g, norm2_g, final_g, w_ada, b_ada, w_in, conv_w,
           da_lam_q1, da_lam_k1, da_lam_q2, da_lam_k2, da_norm_g, rw_w0, rw_w2, rw_a0, rw_a2, rw_g2,
           rw_k_k, rw_k_a, rw_r_k, rw_norm_g, ml_i_bias, ml_f_bias, ml_norm_g, gl_a2, gl_a_bias, gl_norm_g,
           w_branch, w_bgate, w_out, w_router, b_router, w_gu, b_gu, w_dn, b_dn):
    p = dict(rw_w0=rw_w0, rw_w2=rw_w2, rw_a0=rw_a0, rw_a2=rw_a2, rw_g2=rw_g2, rw_k_k=rw_k_k, rw_k_a=rw_k_a,
             rw_r_k=rw_r_k, rw_norm_g=rw_norm_g, ml_i_bias=ml_i_bias, ml_f_bias=ml_f_bias,
             ml_norm_g=ml_norm_g, gl_a2=gl_a2, gl_a_bias=gl_a_bias, gl_norm_g=gl_norm_g)
    n_cb, n_cl, d = x_prompt.shape
    n_db, n_dl, _ = x_sample.shape
    depth = w_in.shape[0]
    n_ctx = n_cb * n_cl
    n_tok = n_ctx + n_db * n_dl
    tm = TOKEN_TILE
    ctx_row = n_db

    def mod_index(i):
        return jnp.where(i < n_ctx // tm, ctx_row, (i - n_ctx // tm) * tm // n_dl)

    n_mod = 16
    cvecs = jnp.zeros((n_mod, d), F32).at[:n_db].set(c).at[ctx_row].set(c_ctx)
    mods = adaln_all(cvecs, w_ada, b_ada)
    mods = jnp.pad(mods.reshape(depth, n_mod, 6, d), ((0, 0), (0, 0), (0, 2), (0, 0)))

    x = jnp.concatenate([x_prompt.reshape(n_ctx, d), x_sample.reshape(n_db * n_dl, d)], axis=0)
    states = []
    for l in range(depth):
        mod = mods[l]
        u, h = in_projection(x, mod, mod_index, norm1_g[l], _permute_w_in(w_in[l]).astype(BF16), tm)
        lam_p = jnp.stack([da_lam_q1[l], da_lam_k1[l], da_lam_q2[l], da_lam_k2[l]])
        n_past = cache_attn_k.shape[2]
        o_da = (attention_ctx(u, lam_p, da_norm_g[l], l, n_cb, n_cl, 0),
                attention_dec(u, cache_attn_k[:, l].reshape(n_db, n_past, 256),
                              cache_attn_v[:, l].reshape(n_db, n_past, 256), lam_p, da_norm_g[l], l,
                              n_db, n_dl, n_ctx))
        o_rw_c, rw_s = rwkv_branch(u, conv_w[l], p, l, n_cb, n_cl, 0)
        o_rw_d, _ = rwkv_branch(u, conv_w[l], p, l, n_db, n_dl, n_ctx, state_rwkv[:, l])
        o_ml_c, ml_c, ml_n, ml_m = mlstm_branch(u, conv_w[l], p, l, n_cb, n_cl, 0)
        o_ml_d = mlstm_branch(u, conv_w[l], p, l, n_db, n_dl, n_ctx,
                              (state_mlstm_c[:, l], state_mlstm_n[:, l], state_mlstm_m[:, l]))[0]
        o_gl_c, gl_s = gla_branch(u, conv_w[l], p, l, n_cb, n_cl, 0)
        o_gl_d, _ = gla_branch(u, conv_w[l], p, l, n_db, n_dl, n_ctx, state_gla[:, l])
        outs_ctx = [o_da[0], o_rw_c, o_ml_c, o_gl_c]
        outs_dec = [o_da[1], o_rw_d, o_ml_d, o_gl_d]
        new_k = u[:n_ctx, U_DA + 256:U_DA + 512].reshape(n_cb, n_cl, DA_H, 2, DA_DK)
        new_v = u[:n_ctx, U_DA + 512:U_DA + 768].reshape(n_cb, n_cl, DA_H, DA_DV)
        states.append((new_k, new_v, rw_s, ml_c, ml_n, ml_m, gl_s))

        x1, h3, route, tile_counts = merge_and_route(x, h, outs_ctx, outs_dec, mod, mod_index,
                                                     w_branch[l].astype(BF16), w_bgate[l].astype(BF16),
                                                     w_out[l].astype(BF16), norm2_g[l], w_router[l],
                                                     b_router[l], tm)
        block_e, row_slot, n_act, row_gate = _route_metadata(route, tile_counts, n_tok)
        ys3 = moe_experts(h3, block_e, row_slot, n_act, row_gate, w_gu[l], b_gu[l], w_dn[l], b_dn[l],
                          n_tok * TOP_K)
        x = moe_combine(ys3, x1, mod, mod_index, final_g, tm, final=(l == depth - 1))

    y_prompt = x[:n_ctx].reshape(n_cb, n_cl, d)
    y_sample = x[n_ctx:].reshape(n_db, n_dl, d)
    stacked = tuple(jnp.stack([s[j] for s in states], axis=1) for j in range(7))
    return (y_prompt, y_sample) + stacked
```
